```python
import math
import jax
import jax.numpy as jnp
from jax import lax
import numpy as np

D_MODEL = 1024
BATCH = 8
SEQ = 8192
DEPTH = 2

CHUNK = 64
N_MIXERS = 2

SSM_EXPAND = 2
D_INNER = SSM_EXPAND * D_MODEL
SSM_HEAD_DIM = 64
SSM_HEADS = D_INNER // SSM_HEAD_DIM
SSM_GROUPS = 8
SSM_HEADS_PER_GROUP = SSM_HEADS // SSM_GROUPS
D_STATE = 128
CONV_WIDTH = 4
CONV_DIM = D_INNER + 2 * SSM_GROUPS * D_STATE
D_IN_PROJ = D_INNER + CONV_DIM + SSM_HEADS

SB_HEAD_DIM = 64
SB_HEADS = D_MODEL // SB_HEAD_DIM
Q_BLOCK = 128

D_FF = -(-8 * D_MODEL // (3 * 256)) * 256

NORM_EPS = 1e-6
GATED_NORM_EPS = 1e-5

kernel_name = "ssd_stickbreaking_interleaved_hybrid"


def rms_norm(x, w, eps=NORM_EPS):
    xf = x.astype(jnp.float32)
    return xf * lax.rsqrt(jnp.mean(xf * xf, axis=-1, keepdims=True) + eps) * w.astype(jnp.float32)


def causal_dwconv(u, w, b):
    k = w.shape[0]
    out = lax.conv_general_dilated(
        u, w[:, None, :], window_strides=(1,), padding=[(k - 1, 0)],
        dimension_numbers=("NWC", "WIO", "NWC"), feature_group_count=u.shape[-1])
    return out + b


def ssd_chunked_scan(xdt, a, b_mat, c_mat):
    bsz, seq = xdt.shape[0], xdt.shape[1]
    nc = seq // CHUNK

    def to_chunks(t):
        return jnp.moveaxis(t.reshape(bsz, nc, CHUNK, *t.shape[2:]), 1, 0)

    causal = jnp.tril(jnp.ones((CHUNK, CHUNK), dtype=bool))

    def step(state, inp):
        xc, ac, bc, cc = inp
        acs = jnp.cumsum(ac, axis=1)
        seg = acs[:, :, None] - acs[:, None, :]
        lmat = jnp.exp(jnp.where(causal[None, :, :, None, None], seg, -jnp.inf))
        cb = jnp.einsum("blgn,bsgn->blsg", cc, bc)
        y_intra = jnp.einsum("blsg,blsgr,bsgrp->blgrp", cb, lmat, xc)
        y_inter = jnp.einsum("blgn,bgrpn->blgrp", cc, state) * jnp.exp(acs)[..., None]
        decay_end = jnp.exp(acs[:, -1:] - acs)
        new_state = (state * jnp.exp(acs[:, -1])[..., None, None]
                     + jnp.einsum("blgn,blgr,blgrp->bgrpn", bc, decay_end, xc))
        return new_state, y_intra + y_inter

    state0 = jnp.zeros((bsz, SSM_GROUPS, SSM_HEADS_PER_GROUP, SSM_HEAD_DIM, D_STATE), jnp.float32)
    _, ys = lax.scan(step, state0, (to_chunks(xdt), to_chunks(a), to_chunks(b_mat), to_chunks(c_mat)))
    return jnp.moveaxis(ys, 0, 1).reshape(xdt.shape)


def mamba2_mixer(h, w_in, conv_w, conv_b, dt_bias, a_log, d_skip, norm_w, w_out):
    bsz, seq, _ = h.shape
    zxbcdt = h @ w_in
    z = zxbcdt[..., :D_INNER]
    xbc = zxbcdt[..., D_INNER:D_INNER + CONV_DIM]
    dt_raw = zxbcdt[..., D_INNER + CONV_DIM:]
    xbc = jax.nn.silu(causal_dwconv(xbc, conv_w, conv_b))
    xs = xbc[..., :D_INNER].astype(jnp.float32).reshape(
        bsz, seq, SSM_GROUPS, SSM_HEADS_PER_GROUP, SSM_HEAD_DIM)
    b_mat = xbc[..., D_INNER:D_INNER + SSM_GROUPS * D_STATE].astype(jnp.float32).reshape(
        bsz, seq, SSM_GROUPS, D_STATE)
    c_mat = xbc[..., D_INNER + SSM_GROUPS * D_STATE:].astype(jnp.float32).reshape(
        bsz, seq, SSM_GROUPS, D_STATE)
    dt = jax.nn.softplus(dt_raw.astype(jnp.float32) + dt_bias.astype(jnp.float32))
    dt = dt.reshape(bsz, seq, SSM_GROUPS, SSM_HEADS_PER_GROUP)
    a_cont = -jnp.exp(a_log.astype(jnp.float32)).reshape(SSM_GROUPS, SSM_HEADS_PER_GROUP)
    y = ssd_chunked_scan(xs * dt[..., None], dt * a_cont, b_mat, c_mat)
    y = y + d_skip.astype(jnp.float32).reshape(SSM_GROUPS, SSM_HEADS_PER_GROUP)[..., None] * xs
    y = y.reshape(bsz, seq, SSM_GROUPS, D_INNER // SSM_GROUPS)
    g = jax.nn.silu(z.astype(jnp.float32)).reshape(y.shape)
    y = rms_norm(y * g, norm_w.reshape(SSM_GROUPS, D_INNER // SSM_GROUPS), GATED_NORM_EPS)
    return y.reshape(bsz, seq, D_INNER).astype(h.dtype) @ w_out


def stick_breaking_block(q_blk, k_pre, v_pre, q_start):
    nq, nk = q_blk.shape[2], k_pre.shape[2]
    z = jnp.einsum("bhqd,bhkd->bhqk", q_blk, k_pre) * (SB_HEAD_DIM ** -0.5)
    q_pos = q_start + jnp.arange(nq)
    k_pos = jnp.arange(nk)
    mask = k_pos[None, :] < q_pos[:, None]
    log_beta = jax.nn.log_sigmoid(z)
    log_keep = jnp.where(mask, jax.nn.log_sigmoid(-z), 0.0)
    log_between = lax.cumsum(log_keep, axis=3, reverse=True) - log_keep
    weights = jnp.where(mask, jnp.exp(log_beta + log_between), 0.0)
    return jnp.einsum("bhqk,bhkd->bhqd", weights, v_pre)


def stick_breaking_mixer(h, w_qkv, q_gain, k_gain, w_o):
    bsz, seq, _ = h.shape
    qkv = (h @ w_qkv).reshape(bsz, seq, 3, SB_HEADS, SB_HEAD_DIM)
    q = jnp.transpose(rms_norm(qkv[:, :, 0], q_gain), (0, 2, 1, 3))
    k = jnp.transpose(rms_norm(qkv[:, :, 1], k_gain), (0, 2, 1, 3))
    v = jnp.transpose(qkv[:, :, 2].astype(jnp.float32), (0, 2, 1, 3))
    outs = []
    for start in range(0, seq, Q_BLOCK):
        end = start + Q_BLOCK
        outs.append(stick_breaking_block(q[:, :, start:end], k[:, :, :end], v[:, :, :end], start))
    o = jnp.concatenate(outs, axis=2)
    o = jnp.transpose(o, (0, 2, 1, 3)).reshape(bsz, seq, D_MODEL).astype(h.dtype)
    return o @ w_o


def swiglu_ffn(h, w_in, w_out):
    gu = h @ w_in
    gate, up = gu[..., :D_FF], gu[..., D_FF:]
    return (jax.nn.silu(gate) * up) @ w_out


def setup_inputs(seed: int = 0) -> dict:
    key = jax.random.key(seed)
    ks = jax.random.split(key, 20)
    n_ssm = (DEPTH + 1) // 2
    n_sb = DEPTH // 2
    f32 = jnp.float32

    def normal(k, shape, scale):
        return jax.random.normal(k, shape, f32) * scale

    x = jax.random.normal(ks[0], (BATCH, SEQ, D_MODEL), f32)
    norm_mix = 1.0 + normal(ks[1], (DEPTH, D_MODEL), 0.05)
    norm_ffn = 1.0 + normal(ks[2], (DEPTH, D_MODEL), 0.05)

    ssm_w_in = normal(ks[3], (n_ssm, D_MODEL, D_IN_PROJ), D_MODEL ** -0.5)
    ssm_conv_w = normal(ks[4], (n_ssm, CONV_WIDTH, CONV_DIM), CONV_WIDTH ** -0.5)
    ssm_conv_b = normal(ks[5], (n_ssm, CONV_DIM), 0.01)
    u = jax.random.uniform(ks[6], (n_ssm, SSM_HEADS), f32)
    dt0 = jnp.exp(u * (math.log(0.1) - math.log(0.001)) + math.log(0.001))
    ssm_dt_bias = dt0 + jnp.log(-jnp.expm1(-dt0))
    ssm_a_log = jnp.log(jax.random.uniform(ks[7], (n_ssm, SSM_HEADS), f32, 1.0, 16.0))
    ssm_d = 1.0 + normal(ks[8], (n_ssm, SSM_HEADS), 0.1)
    ssm_norm_w = 1.0 + normal(ks[9], (n_ssm, D_INNER), 0.05)
    ssm_w_out = normal(ks[10], (n_ssm, D_INNER, D_MODEL), D_INNER ** -0.5)

    sb_w_qkv = normal(ks[11], (n_sb, D_MODEL, 3 * D_MODEL), D_MODEL ** -0.5)
    sb_q_gain = 1.0 + normal(ks[12], (n_sb, SB_HEAD_DIM), 0.05)
    sb_k_gain = 1.0 + normal(ks[13], (n_sb, SB_HEAD_DIM), 0.05)
    sb_w_o = normal(ks[14], (n_sb, D_MODEL, D_MODEL), D_MODEL ** -0.5)

    ffn_w_in = normal(ks[15], (DEPTH, D_MODEL, 2 * D_FF), D_MODEL ** -0.5)
    ffn_w_out = normal(ks[16], (DEPTH, D_FF, D_MODEL), D_FF ** -0.5)

    return {
        "x": x, "norm_mix": norm_mix, "norm_ffn": norm_ffn,
        "ssm_w_in": ssm_w_in, "ssm_conv_w": ssm_conv_w, "ssm_conv_b": ssm_conv_b,
        "ssm_dt_bias": ssm_dt_bias, "ssm_a_log": ssm_a_log, "ssm_d": ssm_d,
        "ssm_norm_w": ssm_norm_w, "ssm_w_out": ssm_w_out,
        "sb_w_qkv": sb_w_qkv, "sb_q_gain": sb_q_gain, "sb_k_gain": sb_k_gain, "sb_w_o": sb_w_o,
        "ffn_w_in": ffn_w_in, "ffn_w_out": ffn_w_out,
    }


def reference(x, norm_mix, norm_ffn, ssm_w_in, ssm_conv_w, ssm_conv_b, ssm_dt_bias,
              ssm_a_log, ssm_d, ssm_norm_w, ssm_w_out, sb_w_qkv, sb_q_gain, sb_k_gain,
              sb_w_o, ffn_w_in, ffn_w_out):
    for i in range(DEPTH):
        h = rms_norm(x, norm_mix[i]).astype(x.dtype)
        j = i // N_MIXERS
        if i % N_MIXERS == 0:
            mix = mamba2_mixer(h, ssm_w_in[j], ssm_conv_w[j], ssm_conv_b[j], ssm_dt_bias[j],
                               ssm_a_log[j], ssm_d[j], ssm_norm_w[j], ssm_w_out[j])
        else:
            mix = stick_breaking_mixer(h, sb_w_qkv[j], sb_q_gain[j], sb_k_gain[j], sb_w_o[j])
        x = x + mix.astype(x.dtype)
        h = rms_norm(x, norm_ffn[i]).astype(x.dtype)
        x = x + swiglu_ffn(h, ffn_w_in[i], ffn_w_out[i]).astype(x.dtype)
    return x
```

```python
import functools

import jax
import jax.numpy as jnp
from jax import lax
from jax.experimental import pallas as pl
from jax.experimental.pallas import tpu as pltpu

F32 = jnp.float32
BF16 = jnp.bfloat16

D_MODEL = 1024
SSM_HEAD_DIM = 64
SSM_HEADS = 32
SSM_GROUPS = 8
SSM_HEADS_PER_GROUP = SSM_HEADS // SSM_GROUPS
D_STATE = 128
D_INNER = SSM_HEADS * SSM_HEAD_DIM
GROUP_CH = D_INNER // SSM_GROUPS
CONV_WIDTH = 4
SB_HEADS = 16
SB_HEAD_DIM = 64
D_FF = 2816
NORM_EPS = 1e-6
GATED_NORM_EPS = 1e-5

LANES = 128
SUBLANES = 8
VMEM_LIMIT_BYTES = 56 * 1024 * 1024

DT_COL = D_INNER + D_INNER + 2 * SSM_GROUPS * D_STATE
IN_PROJ_COLS = DT_COL + LANES


def _rms(x, eps):
    return x * lax.rsqrt(jnp.mean(x * x, axis=-1, keepdims=True) + eps)


def _softplus(x):
    return jnp.maximum(x, 0.0) + jnp.log(1.0 + jnp.exp(-jnp.abs(x)))


def _silu(x):
    return x / (1.0 + jnp.exp(-x))


def _compiler_params(semantics):
    return pltpu.CompilerParams(dimension_semantics=semantics, vmem_limit_bytes=VMEM_LIMIT_BYTES)


def _norm_proj_kernel(x_ref, g_ref, w_ref, o_ref, h_ref):
    @pl.when(pl.program_id(1) == 0)
    def _():
        h_ref[...] = (_rms(x_ref[...], NORM_EPS) * g_ref[...]).astype(BF16)

    o_ref[...] = jnp.dot(h_ref[...], w_ref[...], preferred_element_type=F32).astype(o_ref.dtype)


def _norm_proj(x, gain, w, *, tm, tn, out_dtype):
    t, d = x.shape
    n = w.shape[1]
    return pl.pallas_call(
        _norm_proj_kernel,
        grid=(t // tm, n // tn),
        in_specs=[
            pl.BlockSpec((tm, d), lambda i, j: (i, 0)),
            pl.BlockSpec((1, d), lambda i, j: (0, 0)),
            pl.BlockSpec((d, tn), lambda i, j: (0, j)),
        ],
        out_specs=pl.BlockSpec((tm, tn), lambda i, j: (i, j)),
        out_shape=jax.ShapeDtypeStruct((t, n), out_dtype),
        scratch_shapes=[pltpu.VMEM((tm, d), BF16)],
        compiler_params=_compiler_params(("parallel", "arbitrary")),
        name="norm_proj",
    )(x, gain, w)


def _qkv_proj_kernel(x_ref, g_ref, w_ref, hg_ref, o_ref, h_ref, *, n_norm_tiles):
    j = pl.program_id(1)

    @pl.when(j == 0)
    def _():
        h_ref[...] = (_rms(x_ref[...], NORM_EPS) * g_ref[...]).astype(BF16)

    y = jnp.dot(h_ref[...], w_ref[...], preferred_element_type=F32)

    @pl.when(j < n_norm_tiles)
    def _():
        tn = y.shape[1]
        row = lax.broadcasted_iota(jnp.int32, (tn, tn), 0) // SB_HEAD_DIM
        col = lax.broadcasted_iota(jnp.int32, (tn, tn), 1) // SB_HEAD_DIM
        head_mean = jnp.where(row == col, 1.0 / SB_HEAD_DIM, 0.0).astype(BF16)
        ms = jnp.dot((y * y).astype(BF16), head_mean, preferred_element_type=F32)
        o_ref[...] = (y * lax.rsqrt(ms + NORM_EPS) * hg_ref[...]).astype(o_ref.dtype)

    @pl.when(j >= n_norm_tiles)
    def _():
        o_ref[...] = y.astype(o_ref.dtype)


def _qkv_proj(x, gain, w, head_gain, *, tm, tn):
    t, d = x.shape
    n = w.shape[1]
    return pl.pallas_call(
        functools.partial(_qkv_proj_kernel, n_norm_tiles=(2 * D_MODEL) // tn),
        grid=(t // tm, n // tn),
        in_specs=[
            pl.BlockSpec((tm, d), lambda i, j: (i, 0)),
            pl.BlockSpec((1, d), lambda i, j: (0, 0)),
            pl.BlockSpec((d, tn), lambda i, j: (0, j)),
            pl.BlockSpec((1, tn), lambda i, j: (0, j)),
        ],
        out_specs=pl.BlockSpec((tm, tn), lambda i, j: (i, j)),
        out_shape=jax.ShapeDtypeStruct((t, n), BF16),
        scratch_shapes=[pltpu.VMEM((tm, d), BF16)],
        compiler_params=_compiler_params(("parallel", "arbitrary")),
        name="qkv_proj",
    )(x, gain, w, head_gain)


def _conv_silu(u_ref, tail_ref, buf_ref, w_ref, b_ref, g, lt):
    buf_ref[0:SUBLANES, :] = tail_ref[g]
    buf_ref[SUBLANES:SUBLANES + lt, :] = u_ref[...]
    tail_ref[g] = buf_ref[lt:lt + SUBLANES, :]
    acc = b_ref[...]
    for k in range(CONV_WIDTH):
        start = SUBLANES - (CONV_WIDTH - 1) + k
        acc = acc + w_ref[k:k + 1, :] * buf_ref[start:start + lt, :]
    return _silu(acc)


def _head_select(pieces):
    lane = lax.broadcasted_iota(jnp.int32, (1, LANES), 1)
    lo = lane < SSM_HEAD_DIM
    return jnp.concatenate(
        [jnp.where(lo, pieces[0], pieces[1]), jnp.where(lo, pieces[2], pieces[3])], axis=1)


def _split3_dot(tri_bf16, a):
    a0 = a.astype(BF16)
    r1 = a - a0.astype(F32)
    a1 = r1.astype(BF16)
    a2 = (r1 - a1.astype(F32)).astype(BF16)
    out = jnp.dot(tri_bf16, a0, preferred_element_type=F32)
    out = out + jnp.dot(tri_bf16, a1, preferred_element_type=F32)
    return out + jnp.dot(tri_bf16, a2, preferred_element_type=F32)


def _mamba_core_kernel(z_ref, ux_ref, ub_ref, uc_ref, dt_ref, xres_ref,
                       cwx_ref, cwb_ref, cwc_ref, cbx_ref, cbb_ref, cbc_ref,
                       dtb_ref, alog_ref, dch_ref, nw_ref, wout_ref,
                       o_ref,
                       state_ref, tailx_ref, tailb_ref, tailc_ref,
                       bufx_ref, bufb_ref, bufc_ref, acst_ref, *, lt):
    i = pl.program_id(1)
    g = pl.program_id(2)

    @pl.when(i == 0)
    def _():
        state_ref[g] = jnp.zeros(state_ref.shape[1:], F32)
        tailx_ref[g] = jnp.zeros(tailx_ref.shape[1:], F32)
        tailb_ref[g] = jnp.zeros(tailb_ref.shape[1:], F32)
        tailc_ref[g] = jnp.zeros(tailc_ref.shape[1:], F32)

    xs = _conv_silu(ux_ref, tailx_ref, bufx_ref, cwx_ref, cbx_ref, g, lt)
    bm = _conv_silu(ub_ref, tailb_ref, bufb_ref, cwb_ref, cbb_ref, g, lt)
    cm = _conv_silu(uc_ref, tailc_ref, bufc_ref, cwc_ref, cbc_ref, g, lt)
    bm16 = bm.astype(BF16)
    cm16 = cm.astype(BF16)

    dt = _softplus(dt_ref[...] + dtb_ref[...])
    a = dt * (-jnp.exp(alog_ref[...]))
    row = lax.broadcasted_iota(jnp.int32, (lt, lt), 0)
    col = lax.broadcasted_iota(jnp.int32, (lt, lt), 1)
    causal = row >= col
    acs = _split3_dot(causal.astype(BF16), a)
    acst_ref[...] = acs.T
    acs_last = acs[lt - 1:lt, :]

    lane = lax.broadcasted_iota(jnp.int32, (1, LANES), 1)
    dt_cols, acs_cols, acs_rows, last_cols = [], [], [], []
    for r in range(SSM_HEADS_PER_GROUP):
        h = g * SSM_HEADS_PER_GROUP + r
        sel = lane == h
        dt_cols.append(jnp.sum(jnp.where(sel, dt, 0.0), axis=1, keepdims=True))
        acs_cols.append(jnp.sum(jnp.where(sel, acs, 0.0), axis=1, keepdims=True))
        last_cols.append(jnp.sum(jnp.where(sel, acs_last, 0.0), axis=1, keepdims=True))
        acs_rows.append(acst_ref[pl.ds(h, 1), :])

    dt_x = _head_select(dt_cols)
    acs_x = _head_select(acs_cols)
    last_x = _head_select(last_cols)
    xdt = xs * dt_x

    cb = lax.dot_general(cm16, bm16, (((1,), (1,)), ((), ())), preferred_element_type=F32)
    lane2 = lax.broadcasted_iota(jnp.int32, (1, GROUP_CH), 1) // SSM_HEAD_DIM
    y = jnp.zeros((lt, GROUP_CH), F32)
    for r in range(SSM_HEADS_PER_GROUP):
        seg = acs_cols[r] - acs_rows[r]
        lmat = jnp.exp(jnp.where(causal, seg, -jnp.inf))
        gm = (cb * lmat).astype(BF16)
        xdt_r = jnp.where(lane2 == r, xdt, 0.0).astype(BF16)
        y = y + jnp.dot(gm, xdt_r, preferred_element_type=F32)

    state = state_ref[g]
    y = y + jnp.dot(cm16, state.astype(BF16), preferred_element_type=F32) * jnp.exp(acs_x)
    y = y + dch_ref[...] * xs

    w_end = (xdt * jnp.exp(last_x - acs_x)).astype(BF16)
    upd = lax.dot_general(bm16, w_end, (((0,), (0,)), ((), ())), preferred_element_type=F32)
    state_ref[g] = state * jnp.exp(last_x) + upd

    yg = y * _silu(z_ref[...])
    yn = (_rms(yg, GATED_NORM_EPS) * nw_ref[...]).astype(BF16)
    contrib = jnp.dot(yn, wout_ref[...], preferred_element_type=F32)

    @pl.when(g == 0)
    def _():
        o_ref[...] = xres_ref[...] + contrib

    @pl.when(g != 0)
    def _():
        o_ref[...] += contrib


def _mamba_core(proj, x_res, conv_w, conv_b, dt_bias, a_log, d_ch, norm_w, w_out, *, batch, lt):
    t = proj.shape[0]
    nt = t // batch // lt
    gch = GROUP_CH // LANES

    def rows(b, i, g):
        return b * nt + i

    x_off = D_INNER // GROUP_CH
    b_off = 2 * D_INNER // D_STATE
    c_off = b_off + SSM_GROUPS
    dt_off = DT_COL // LANES
    in_specs = [
        pl.BlockSpec((lt, GROUP_CH), lambda b, i, g: (rows(b, i, g), g)),
        pl.BlockSpec((lt, GROUP_CH), lambda b, i, g: (rows(b, i, g), x_off + g)),
        pl.BlockSpec((lt, D_STATE), lambda b, i, g: (rows(b, i, g), b_off + g)),
        pl.BlockSpec((lt, D_STATE), lambda b, i, g: (rows(b, i, g), c_off + g)),
        pl.BlockSpec((lt, LANES), lambda b, i, g: (rows(b, i, g), dt_off)),
        pl.BlockSpec((lt, D_MODEL), lambda b, i, g: (rows(b, i, g), 0)),
        pl.BlockSpec((CONV_WIDTH, GROUP_CH), lambda b, i, g: (0, g)),
        pl.BlockSpec((CONV_WIDTH, D_STATE), lambda b, i, g: (0, b_off - x_off * gch + g)),
        pl.BlockSpec((CONV_WIDTH, D_STATE), lambda b, i, g: (0, c_off - x_off * gch + g)),
        pl.BlockSpec((1, GROUP_CH), lambda b, i, g: (0, g)),
        pl.BlockSpec((1, D_STATE), lambda b, i, g: (0, b_off - x_off * gch + g)),
        pl.BlockSpec((1, D_STATE), lambda b, i, g: (0, c_off - x_off * gch + g)),
        pl.BlockSpec((1, LANES), lambda b, i, g: (0, 0)),
        pl.BlockSpec((1, LANES), lambda b, i, g: (0, 0)),
        pl.BlockSpec((1, GROUP_CH), lambda b, i, g: (0, g)),
        pl.BlockSpec((1, GROUP_CH), lambda b, i, g: (0, g)),
        pl.BlockSpec((GROUP_CH, D_MODEL), lambda b, i, g: (g, 0)),
    ]
    return pl.pallas_call(
        functools.partial(_mamba_core_kernel, lt=lt),
        grid=(batch, nt, SSM_GROUPS),
        in_specs=in_specs,
        out_specs=pl.BlockSpec((lt, D_MODEL), lambda b, i, g: (rows(b, i, g), 0)),
        out_shape=jax.ShapeDtypeStruct((t, D_MODEL), F32),
        scratch_shapes=[
            pltpu.VMEM((SSM_GROUPS, D_STATE, GROUP_CH), F32),
            pltpu.VMEM((SSM_GROUPS, SUBLANES, GROUP_CH), F32),
            pltpu.VMEM((SSM_GROUPS, SUBLANES, D_STATE), F32),
            pltpu.VMEM((SSM_GROUPS, SUBLANES, D_STATE), F32),
            pltpu.VMEM((lt + SUBLANES, GROUP_CH), F32),
            pltpu.VMEM((lt + SUBLANES, D_STATE), F32),
            pltpu.VMEM((lt + SUBLANES, D_STATE), F32),
            pltpu.VMEM((LANES, lt), F32),
        ],
        compiler_params=_compiler_params(("parallel", "arbitrary", "arbitrary")),
        name="mamba_core",
    )(proj, proj, proj, proj, proj, x_res,
      conv_w, conv_w, conv_w, conv_b, conv_b, conv_b,
      dt_bias, a_log, d_ch, norm_w, w_out)


def _sb_attn_kernel(q_ref, k_ref, v_ref, o_ref, *, blk):
    qi = pl.program_id(2)
    q = q_ref[0]
    lane = lax.broadcasted_iota(jnp.int32, (1, LANES), 1)
    row = lax.broadcasted_iota(jnp.int32, (blk, blk), 0)
    col = lax.broadcasted_iota(jnp.int32, (blk, blk), 1)
    later = (row > col).astype(BF16)
    strictly_causal = col < row

    def visit(j, qm, carry, acc, masked):
        start = pl.multiple_of(j * blk, blk)
        kj = k_ref[0, pl.ds(start, blk), :]
        vj = v_ref[0, pl.ds(start, blk), :]
        z = lax.dot_general(qm, kj, (((1,), (1,)), ((), ())), preferred_element_type=F32)
        log_keep = -_softplus(z)
        if masked:
            log_keep = jnp.where(strictly_causal, log_keep, 0.0)
        between = jnp.dot(log_keep.astype(BF16), later, preferred_element_type=F32)
        w = jnp.exp(z + log_keep + between + carry)
        if masked:
            w = jnp.where(strictly_causal, w, 0.0)
        acc = acc + jnp.dot(w.astype(BF16), vj, preferred_element_type=F32)
        carry = carry + between[:, 0:1] + log_keep[:, 0:1]
        return carry, acc

    accs = []
    for h in range(2):
        in_head = (lane // SB_HEAD_DIM) == h
        qm = jnp.where(in_head, q, jnp.zeros_like(q))
        carry, acc = visit(qi, qm, jnp.zeros((blk, 1), F32), jnp.zeros((blk, LANES), F32), True)

        def body(t, c, qm=qm):
            return visit(qi - 1 - t, qm, c[0], c[1], False)

        carry, acc = lax.fori_loop(0, qi, body, (carry, acc))
        accs.append(acc)
    o_ref[0] = jnp.where(lane < SB_HEAD_DIM, accs[0], accs[1]).astype(o_ref.dtype)


def _sb_attention(qkv, *, blk):
    b, l, _ = qkv.shape
    pairs = SB_HEADS * SB_HEAD_DIM // LANES
    return pl.pallas_call(
        functools.partial(_sb_attn_kernel, blk=blk),
        grid=(b, pairs, l // blk),
        in_specs=[
            pl.BlockSpec((1, blk, LANES), lambda bi, p, i: (bi, i, p)),
            pl.BlockSpec((1, l, LANES), lambda bi, p, i: (bi, 0, pairs + p)),
            pl.BlockSpec((1, l, LANES), lambda bi, p, i: (bi, 0, 2 * pairs + p)),
        ],
        out_specs=pl.BlockSpec((1, blk, LANES), lambda bi, p, i: (bi, i, p)),
        out_shape=jax.ShapeDtypeStruct((b, l, D_MODEL), BF16),
        compiler_params=_compiler_params(("parallel", "parallel", "arbitrary")),
        name="sb_attention",
    )(qkv, qkv, qkv)


def _proj_residual_kernel(x_ref, a_ref, w_ref, o_ref):
    o_ref[...] = x_ref[...] + jnp.dot(a_ref[...], w_ref[...], preferred_element_type=F32)


def _proj_residual(x, a, w, *, tm):
    t, d = x.shape
    k = a.shape[1]
    return pl.pallas_call(
        _proj_residual_kernel,
        grid=(t // tm,),
        in_specs=[
            pl.BlockSpec((tm, d), lambda i: (i, 0)),
            pl.BlockSpec((tm, k), lambda i: (i, 0)),
            pl.BlockSpec((k, d), lambda i: (0, 0)),
        ],
        out_specs=pl.BlockSpec((tm, d), lambda i: (i, 0)),
        out_shape=jax.ShapeDtypeStruct((t, d), F32),
        compiler_params=_compiler_params(("parallel",)),
        name="proj_residual",
    )(x, a, w)


def _ffn_kernel(x_ref, g_ref, wg_ref, wu_ref, wo_ref, o_ref, h_ref):
    j = pl.program_id(1)

    @pl.when(j == 0)
    def _():
        x = x_ref[...]
        h_ref[...] = (_rms(x, NORM_EPS) * g_ref[...]).astype(BF16)
        o_ref[...] = x

    h = h_ref[...]
    gate = jnp.dot(h, wg_ref[...], preferred_element_type=F32)
    up = jnp.dot(h, wu_ref[...], preferred_element_type=F32)
    act = (_silu(gate) * up).astype(BF16)
    o_ref[...] += jnp.dot(act, wo_ref[...], preferred_element_type=F32)


def _ffn(x, gain, w_in, w_out, *, tm, tf):
    t, d = x.shape
    nf = D_FF // tf
    return pl.pallas_call(
        _ffn_kernel,
        grid=(t // tm, nf),
        in_specs=[
            pl.BlockSpec((tm, d), lambda i, j: (i, 0)),
            pl.BlockSpec((1, d), lambda i, j: (0, 0)),
            pl.BlockSpec((d, tf), lambda i, j: (0, j)),
            pl.BlockSpec((d, tf), lambda i, j: (0, nf + j)),
            pl.BlockSpec((tf, d), lambda i, j: (j, 0)),
        ],
        out_specs=pl.BlockSpec((tm, d), lambda i, j: (i, 0)),
        out_shape=jax.ShapeDtypeStruct((t, d), F32),
        scratch_shapes=[pltpu.VMEM((tm, d), BF16)],
        compiler_params=_compiler_params(("parallel", "arbitrary")),
        name="swiglu_ffn",
    )(x, gain, w_in, w_in, w_out)


def _tiles(batch, seq):
    t = batch * seq
    tm = min(1024, t)
    return dict(
        proj_tm=tm,
        in_proj_tn=IN_PROJ_COLS // 7,
        qkv_tn=512,
        mamba_lt=min(256, seq),
        attn_blk=min(256, seq),
        ffn_tm=min(512, t),
        ffn_tf=D_FF // 2,
    )


def kernel(x, norm_mix, norm_ffn, ssm_w_in, ssm_conv_w, ssm_conv_b, ssm_dt_bias, ssm_a_log, ssm_d,
           ssm_norm_w, ssm_w_out, sb_w_qkv, sb_q_gain, sb_k_gain, sb_w_o, ffn_w_in, ffn_w_out):
    batch, seq, d = x.shape
    t = batch * seq
    tl = _tiles(batch, seq)
    xf = x.reshape(t, d)

    w_in = jnp.pad(ssm_w_in[0], ((0, 0), (0, IN_PROJ_COLS - ssm_w_in.shape[2]))).astype(BF16)
    proj = _norm_proj(xf, norm_mix[0:1], w_in, tm=tl["proj_tm"], tn=tl["in_proj_tn"], out_dtype=F32)
    pad_heads = LANES - SSM_HEADS
    xf = _mamba_core(
        proj, xf, ssm_conv_w[0], ssm_conv_b[0:1],
        jnp.pad(ssm_dt_bias[0:1], ((0, 0), (0, pad_heads))),
        jnp.pad(ssm_a_log[0:1], ((0, 0), (0, pad_heads))),
        jnp.repeat(ssm_d[0:1], SSM_HEAD_DIM, axis=1),
        ssm_norm_w[0:1], ssm_w_out[0].astype(BF16),
        batch=batch, lt=tl["mamba_lt"])
    xf = _ffn(xf, norm_ffn[0:1], ffn_w_in[0].astype(BF16), ffn_w_out[0].astype(BF16),
              tm=tl["ffn_tm"], tf=tl["ffn_tf"])

    head_gain = jnp.concatenate([
        jnp.tile(sb_q_gain[0], SB_HEADS) * (SB_HEAD_DIM ** -0.5),
        jnp.tile(sb_k_gain[0], SB_HEADS),
        jnp.ones((D_MODEL,), F32)])[None, :]
    qkv = _qkv_proj(xf, norm_mix[1:2], sb_w_qkv[0].astype(BF16), head_gain,
                    tm=tl["proj_tm"], tn=tl["qkv_tn"])
    attn = _sb_attention(qkv.reshape(batch, seq, 3 * d), blk=tl["attn_blk"])
    xf = _proj_residual(xf, attn.reshape(t, d), sb_w_o[0].astype(BF16), tm=tl["proj_tm"])
    xf = _ffn(xf, norm_ffn[1:2], ffn_w_in[1].astype(BF16), ffn_w_out[1].astype(BF16),
              tm=tl["ffn_tm"], tf=tl["ffn_tf"])
    return xf.reshape(batch, seq, d)
```

```python
import functools

import jax
import jax.numpy as jnp
from jax import lax
from jax.experimental import pallas as pl
from jax.experimental.pallas import tpu as pltpu

F32 = jnp.float32
BF16 = jnp.bfloat16

D_MODEL = 1024
SSM_HEAD_DIM = 64
SSM_HEADS = 32
SSM_GROUPS = 8
SSM_HEADS_PER_GROUP = SSM_HEADS // SSM_GROUPS
D_STATE = 128
D_INNER = SSM_HEADS * SSM_HEAD_DIM
GROUP_CH = D_INNER // SSM_GROUPS
CONV_WIDTH = 4
SB_HEADS = 16
SB_HEAD_DIM = 64
D_FF = 2816
NORM_EPS = 1e-6
GATED_NORM_EPS = 1e-5
LOG2E = 1.4426950408889634

LANES = 128
SUBLANES = 8
VMEM_LIMIT_BYTES = 56 * 1024 * 1024

DT_COL = D_INNER + D_INNER + 2 * SSM_GROUPS * D_STATE
IN_PROJ_COLS = DT_COL + LANES


def _rms(x, eps):
    return x * lax.rsqrt(jnp.mean(x * x, axis=-1, keepdims=True) + eps)


def _softplus(x):
    return jnp.maximum(x, 0.0) + jnp.log(1.0 + jnp.exp(-jnp.abs(x)))


def _silu(x):
    return x / (1.0 + jnp.exp(-x))


def _compiler_params(semantics):
    return pltpu.CompilerParams(dimension_semantics=semantics, vmem_limit_bytes=VMEM_LIMIT_BYTES)


def _norm_proj_kernel(x_ref, g_ref, w_ref, o_ref, h_ref):
    @pl.when(pl.program_id(1) == 0)
    def _():
        h_ref[...] = (_rms(x_ref[...], NORM_EPS) * g_ref[...]).astype(BF16)

    o_ref[...] = jnp.dot(h_ref[...], w_ref[...], preferred_element_type=F32).astype(o_ref.dtype)


def _norm_proj(x, gain, w, *, tm, tn, out_dtype):
    t, d = x.shape
    n = w.shape[1]
    return pl.pallas_call(
        _norm_proj_kernel,
        grid=(t // tm, n // tn),
        in_specs=[
            pl.BlockSpec((tm, d), lambda i, j: (i, 0)),
            pl.BlockSpec((1, d), lambda i, j: (0, 0)),
            pl.BlockSpec((d, tn), lambda i, j: (0, j)),
        ],
        out_specs=pl.BlockSpec((tm, tn), lambda i, j: (i, j)),
        out_shape=jax.ShapeDtypeStruct((t, n), out_dtype),
        scratch_shapes=[pltpu.VMEM((tm, d), BF16)],
        compiler_params=_compiler_params(("parallel", "arbitrary")),
        name="norm_proj",
    )(x, gain, w)


def _qkv_proj_kernel(x_ref, g_ref, w_ref, hg_ref, o_ref, h_ref, *, n_norm_tiles):
    j = pl.program_id(1)

    @pl.when(j == 0)
    def _():
        h_ref[...] = (_rms(x_ref[...], NORM_EPS) * g_ref[...]).astype(BF16)

    y = jnp.dot(h_ref[...], w_ref[...], preferred_element_type=F32)

    @pl.when(j < n_norm_tiles)
    def _():
        tn = y.shape[1]
        row = lax.broadcasted_iota(jnp.int32, (tn, tn), 0) // SB_HEAD_DIM
        col = lax.broadcasted_iota(jnp.int32, (tn, tn), 1) // SB_HEAD_DIM
        head_mean = jnp.where(row == col, 1.0 / SB_HEAD_DIM, 0.0).astype(BF16)
        ms = jnp.dot((y * y).astype(BF16), head_mean, preferred_element_type=F32)
        o_ref[...] = (y * lax.rsqrt(ms + NORM_EPS) * hg_ref[...]).astype(o_ref.dtype)

    @pl.when(j >= n_norm_tiles)
    def _():
        o_ref[...] = y.astype(o_ref.dtype)


def _qkv_proj(x, gain, w, head_gain, *, tm, tn):
    t, d = x.shape
    n = w.shape[1]
    return pl.pallas_call(
        functools.partial(_qkv_proj_kernel, n_norm_tiles=(2 * D_MODEL) // tn),
        grid=(t // tm, n // tn),
        in_specs=[
            pl.BlockSpec((tm, d), lambda i, j: (i, 0)),
            pl.BlockSpec((1, d), lambda i, j: (0, 0)),
            pl.BlockSpec((d, tn), lambda i, j: (0, j)),
            pl.BlockSpec((1, tn), lambda i, j: (0, j)),
        ],
        out_specs=pl.BlockSpec((tm, tn), lambda i, j: (i, j)),
        out_shape=jax.ShapeDtypeStruct((t, n), BF16),
        scratch_shapes=[pltpu.VMEM((tm, d), BF16)],
        compiler_params=_compiler_params(("parallel", "arbitrary")),
        name="qkv_proj",
    )(x, gain, w, head_gain)


def _conv_silu(u_ref, tail_ref, buf_ref, w_ref, b_ref, g, lt):
    buf_ref[0:SUBLANES, :] = tail_ref[g]
    buf_ref[SUBLANES:SUBLANES + lt, :] = u_ref[...]
    tail_ref[g] = buf_ref[lt:lt + SUBLANES, :]
    acc = b_ref[...]
    for k in range(CONV_WIDTH):
        start = SUBLANES - (CONV_WIDTH - 1) + k
        acc = acc + w_ref[k:k + 1, :] * buf_ref[start:start + lt, :]
    return _silu(acc)


def _head_select(pieces):
    lane = lax.broadcasted_iota(jnp.int32, (1, LANES), 1)
    lo = lane < SSM_HEAD_DIM
    return jnp.concatenate(
        [jnp.where(lo, pieces[0], pieces[1]), jnp.where(lo, pieces[2], pieces[3])], axis=1)


def _split3_dot(tri_bf16, a):
    a0 = a.astype(BF16)
    r1 = a - a0.astype(F32)
    a1 = r1.astype(BF16)
    a2 = (r1 - a1.astype(F32)).astype(BF16)
    out = jnp.dot(tri_bf16, a0, preferred_element_type=F32)
    out = out + jnp.dot(tri_bf16, a1, preferred_element_type=F32)
    return out + jnp.dot(tri_bf16, a2, preferred_element_type=F32)


def _mamba_core_kernel(z_ref, ux_ref, ub_ref, uc_ref, dt_ref, xres_ref,
                       cwx_ref, cwb_ref, cwc_ref, cbx_ref, cbb_ref, cbc_ref,
                       dtb_ref, alog_ref, dch_ref, nw_ref, wout_ref,
                       o_ref,
                       state_ref, tailx_ref, tailb_ref, tailc_ref,
                       bufx_ref, bufb_ref, bufc_ref, acst_ref, *, lt):
    i = pl.program_id(1)
    g = pl.program_id(2)

    @pl.when(i == 0)
    def _():
        state_ref[g] = jnp.zeros(state_ref.shape[1:], F32)
        tailx_ref[g] = jnp.zeros(tailx_ref.shape[1:], F32)
        tailb_ref[g] = jnp.zeros(tailb_ref.shape[1:], F32)
        tailc_ref[g] = jnp.zeros(tailc_ref.shape[1:], F32)

    xs = _conv_silu(ux_ref, tailx_ref, bufx_ref, cwx_ref, cbx_ref, g, lt)
    bm = _conv_silu(ub_ref, tailb_ref, bufb_ref, cwb_ref, cbb_ref, g, lt)
    cm = _conv_silu(uc_ref, tailc_ref, bufc_ref, cwc_ref, cbc_ref, g, lt)
    bm16 = bm.astype(BF16)
    cm16 = cm.astype(BF16)

    dt = _softplus(dt_ref[...] + dtb_ref[...])
    a = dt * (-jnp.exp(alog_ref[...]))
    row = lax.broadcasted_iota(jnp.int32, (lt, lt), 0)
    col = lax.broadcasted_iota(jnp.int32, (lt, lt), 1)
    causal = row >= col
    acs = _split3_dot(causal.astype(BF16), a)
    acst_ref[...] = acs.T
    acs_last = acs[lt - 1:lt, :]

    lane = lax.broadcasted_iota(jnp.int32, (1, LANES), 1)
    dt_cols, acs_cols, acs_rows, last_cols = [], [], [], []
    for r in range(SSM_HEADS_PER_GROUP):
        h = g * SSM_HEADS_PER_GROUP + r
        sel = lane == h
        dt_cols.append(jnp.sum(jnp.where(sel, dt, 0.0), axis=1, keepdims=True))
        acs_cols.append(jnp.sum(jnp.where(sel, acs, 0.0), axis=1, keepdims=True))
        last_cols.append(jnp.sum(jnp.where(sel, acs_last, 0.0), axis=1, keepdims=True))
        acs_rows.append(acst_ref[pl.ds(h, 1), :])

    dt_x = _head_select(dt_cols)
    acs_x = _head_select(acs_cols)
    last_x = _head_select(last_cols)
    xdt = xs * dt_x

    cb = lax.dot_general(cm16, bm16, (((1,), (1,)), ((), ())), preferred_element_type=F32)
    lane2 = lax.broadcasted_iota(jnp.int32, (1, GROUP_CH), 1) // SSM_HEAD_DIM
    y = jnp.zeros((lt, GROUP_CH), F32)
    for r in range(SSM_HEADS_PER_GROUP):
        seg = acs_cols[r] - acs_rows[r]
        lmat = jnp.exp(jnp.where(causal, seg, -jnp.inf))
        gm = (cb * lmat).astype(BF16)
        xdt_r = jnp.where(lane2 == r, xdt, 0.0).astype(BF16)
        y = y + jnp.dot(gm, xdt_r, preferred_element_type=F32)

    state = state_ref[g]
    y = y + jnp.dot(cm16, state.astype(BF16), preferred_element_type=F32) * jnp.exp(acs_x)
    y = y + dch_ref[...] * xs

    w_end = (xdt * jnp.exp(last_x - acs_x)).astype(BF16)
    upd = lax.dot_general(bm16, w_end, (((0,), (0,)), ((), ())), preferred_element_type=F32)
    state_ref[g] = state * jnp.exp(last_x) + upd

    yg = y * _silu(z_ref[...])
    yn = (_rms(yg, GATED_NORM_EPS) * nw_ref[...]).astype(BF16)
    contrib = jnp.dot(yn, wout_ref[...], preferred_element_type=F32)

    @pl.when(g == 0)
    def _():
        o_ref[...] = xres_ref[...] + contrib

    @pl.when(g != 0)
    def _():
        o_ref[...] += contrib


def _mamba_core(proj, x_res, conv_w, conv_b, dt_bias, a_log, d_ch, norm_w, w_out, *, batch, lt):
    t = proj.shape[0]
    nt = t // batch // lt
    gch = GROUP_CH // LANES

    def rows(b, i, g):
        return b * nt + i

    x_off = D_INNER // GROUP_CH
    b_off = 2 * D_INNER // D_STATE
    c_off = b_off + SSM_GROUPS
    dt_off = DT_COL // LANES
    in_specs = [
        pl.BlockSpec((lt, GROUP_CH), lambda b, i, g: (rows(b, i, g), g)),
        pl.BlockSpec((lt, GROUP_CH), lambda b, i, g: (rows(b, i, g), x_off + g)),
        pl.BlockSpec((lt, D_STATE), lambda b, i, g: (rows(b, i, g), b_off + g)),
        pl.BlockSpec((lt, D_STATE), lambda b, i, g: (rows(b, i, g), c_off + g)),
        pl.BlockSpec((lt, LANES), lambda b, i, g: (rows(b, i, g), dt_off)),
        pl.BlockSpec((lt, D_MODEL), lambda b, i, g: (rows(b, i, g), 0)),
        pl.BlockSpec((CONV_WIDTH, GROUP_CH), lambda b, i, g: (0, g)),
        pl.BlockSpec((CONV_WIDTH, D_STATE), lambda b, i, g: (0, b_off - x_off * gch + g)),
        pl.BlockSpec((CONV_WIDTH, D_STATE), lambda b, i, g: (0, c_off - x_off * gch + g)),
        pl.BlockSpec((1, GROUP_CH), lambda b, i, g: (0, g)),
        pl.BlockSpec((1, D_STATE), lambda b, i, g: (0, b_off - x_off * gch + g)),
        pl.BlockSpec((1, D_STATE), lambda b, i, g: (0, c_off - x_off * gch + g)),
        pl.BlockSpec((1, LANES), lambda b, i, g: (0, 0)),
        pl.BlockSpec((1, LANES), lambda b, i, g: (0, 0)),
        pl.BlockSpec((1, GROUP_CH), lambda b, i, g: (0, g)),
        pl.BlockSpec((1, GROUP_CH), lambda b, i, g: (0, g)),
        pl.BlockSpec((GROUP_CH, D_MODEL), lambda b, i, g: (g, 0)),
    ]
    return pl.pallas_call(
        functools.partial(_mamba_core_kernel, lt=lt),
        grid=(batch, nt, SSM_GROUPS),
        in_specs=in_specs,
        out_specs=pl.BlockSpec((lt, D_MODEL), lambda b, i, g: (rows(b, i, g), 0)),
        out_shape=jax.ShapeDtypeStruct((t, D_MODEL), F32),
        scratch_shapes=[
            pltpu.VMEM((SSM_GROUPS, D_STATE, GROUP_CH), F32),
            pltpu.VMEM((SSM_GROUPS, SUBLANES, GROUP_CH), F32),
            pltpu.VMEM((SSM_GROUPS, SUBLANES, D_STATE), F32),
            pltpu.VMEM((SSM_GROUPS, SUBLANES, D_STATE), F32),
            pltpu.VMEM((lt + SUBLANES, GROUP_CH), F32),
            pltpu.VMEM((lt + SUBLANES, D_STATE), F32),
            pltpu.VMEM((lt + SUBLANES, D_STATE), F32),
            pltpu.VMEM((LANES, lt), F32),
        ],
        compiler_params=_compiler_params(("parallel", "arbitrary", "arbitrary")),
        name="mamba_core",
    )(proj, proj, proj, proj, proj, x_res,
      conv_w, conv_w, conv_w, conv_b, conv_b, conv_b,
      dt_bias, a_log, d_ch, norm_w, w_out)


def _softplus2(z2):
    neg_abs = lax.bitcast_convert_type(
        lax.bitcast_convert_type(z2, jnp.uint32) | jnp.uint32(0x80000000), F32)
    return jnp.maximum(z2, 0.0) + jnp.log(1.0 + jnp.exp2(neg_abs)) * LOG2E


def _sb_attn_kernel(q_ref, k_ref, v_ref, o_ref, qs_ref, carry_ref, acc_ref, zs_ref, *, blk):
    qi = pl.program_id(2)
    q = q_ref[0]
    lane = lax.broadcasted_iota(jnp.int32, (1, LANES), 1)
    first_head = lane < SB_HEAD_DIM
    zero = jnp.zeros_like(q)
    q0 = jnp.where(first_head, q, zero)
    q1 = jnp.where(first_head, zero, q)
    qs_ref[...] = jnp.concatenate([q0[:blk], q1[:blk], q0[blk:], q1[blk:]], axis=0)

    row = lax.broadcasted_iota(jnp.int32, (blk, blk), 0)
    col = lax.broadcasted_iota(jnp.int32, (blk, blk), 1)
    neg_suffix = jnp.where(row >= col, -1.0, 0.0).astype(BF16)
    row2 = lax.broadcasted_iota(jnp.int32, (2 * blk, blk), 0) & (blk - 1)
    col2 = lax.broadcasted_iota(jnp.int32, (2 * blk, blk), 1)
    diag_causal = col2 < row2

    def logits(q_rows, j):
        kj = k_ref[0, pl.ds(pl.multiple_of(j * blk, blk), blk), :]
        return lax.dot_general(q_rows, kj, (((1,), (1,)), ((), ())), preferred_element_type=F32)

    def weigh(z, j, carry, mask):
        vj = v_ref[0, pl.ds(pl.multiple_of(j * blk, blk), blk), :]
        sp = _softplus2(z)
        if mask is not None:
            sp = jnp.where(mask, sp, 0.0)
        incl = jnp.dot(sp.astype(BF16), neg_suffix, preferred_element_type=F32)
        w = jnp.exp2(z + incl + carry)
        if mask is not None:
            w = jnp.where(mask, w, 0.0)
        return carry + incl[:, 0:1], jnp.dot(w.astype(BF16), vj, preferred_element_type=F32)

    top = qs_ref[0:2 * blk, :]
    bottom = qs_ref[2 * blk:4 * blk, :]
    no_carry = jnp.zeros((2 * blk, 1), F32)
    jd = 2 * qi
    c_bot, a_bot = weigh(logits(bottom, jd + 1), jd + 1, no_carry, diag_causal)
    c_bot, a_bot2 = weigh(logits(bottom, jd), jd, c_bot, None)
    c_top, a_top = weigh(logits(top, jd), jd, no_carry, diag_causal)
    carry_ref[0:2 * blk, :] = c_top
    carry_ref[2 * blk:4 * blk, :] = c_bot
    acc_ref[0:2 * blk, :] = a_top
    acc_ref[2 * blk:4 * blk, :] = a_bot + a_bot2

    zs_ref[...] = logits(qs_ref[...], jnp.maximum(jd - 1, 0))

    def body(t, _):
        j = jd - 1 - 2 * t
        z0 = zs_ref[...]
        z1 = logits(qs_ref[...], j - 1)
        c, a = weigh(z0, j, carry_ref[...], None)
        zs_ref[...] = logits(qs_ref[...], jnp.maximum(j - 2, 0))
        c, a2 = weigh(z1, j - 1, c, None)
        carry_ref[...] = c
        acc_ref[...] += a + a2
        return 0

    lax.fori_loop(0, qi, body, 0)
    acc = acc_ref[...]
    o_ref[0, 0:blk, :] = jnp.where(first_head, acc[0:blk], acc[blk:2 * blk]).astype(o_ref.dtype)
    o_ref[0, blk:2 * blk, :] = jnp.where(first_head, acc[2 * blk:3 * blk], acc[3 * blk:]).astype(o_ref.dtype)


def _sb_attention(qkv, *, blk):
    b, l, _ = qkv.shape
    pairs = SB_HEADS * SB_HEAD_DIM // LANES
    return pl.pallas_call(
        functools.partial(_sb_attn_kernel, blk=blk),
        grid=(b, pairs, l // (2 * blk)),
        in_specs=[
            pl.BlockSpec((1, 2 * blk, LANES), lambda bi, p, i: (bi, i, p)),
            pl.BlockSpec((1, l, LANES), lambda bi, p, i: (bi, 0, pairs + p)),
            pl.BlockSpec((1, l, LANES), lambda bi, p, i: (bi, 0, 2 * pairs + p)),
        ],
        out_specs=pl.BlockSpec((1, 2 * blk, LANES), lambda bi, p, i: (bi, i, p)),
        out_shape=jax.ShapeDtypeStruct((b, l, D_MODEL), BF16),
        scratch_shapes=[
            pltpu.VMEM((4 * blk, LANES), BF16),
            pltpu.VMEM((4 * blk, 1), F32),
            pltpu.VMEM((4 * blk, LANES), F32),
            pltpu.VMEM((4 * blk, blk), F32),
        ],
        compiler_params=_compiler_params(("parallel", "parallel", "arbitrary")),
        name="sb_attention",
    )(qkv, qkv, qkv)


def _proj_residual_kernel(x_ref, a_ref, w_ref, o_ref):
    o_ref[...] = x_ref[...] + jnp.dot(a_ref[...], w_ref[...], preferred_element_type=F32)


def _proj_residual(x, a, w, *, tm):
    t, d = x.shape
    k = a.shape[1]
    return pl.pallas_call(
        _proj_residual_kernel,
        grid=(t // tm,),
        in_specs=[
            pl.BlockSpec((tm, d), lambda i: (i, 0)),
            pl.BlockSpec((tm, k), lambda i: (i, 0)),
            pl.BlockSpec((k, d), lambda i: (0, 0)),
        ],
        out_specs=pl.BlockSpec((tm, d), lambda i: (i, 0)),
        out_shape=jax.ShapeDtypeStruct((t, d), F32),
        compiler_params=_compiler_params(("parallel",)),
        name="proj_residual",
    )(x, a, w)


def _ffn_kernel(x_ref, g_ref, wg_ref, wu_ref, wo_ref, o_ref, h_ref):
    j = pl.program_id(1)

    @pl.when(j == 0)
    def _():
        x = x_ref[...]
        h_ref[...] = (_rms(x, NORM_EPS) * g_ref[...]).astype(BF16)
        o_ref[...] = x

    h = h_ref[...]
    gate = jnp.dot(h, wg_ref[...], preferred_element_type=F32)
    up = jnp.dot(h, wu_ref[...], preferred_element_type=F32)
    act = (_silu(gate) * up).astype(BF16)
    o_ref[...] += jnp.dot(act, wo_ref[...], preferred_element_type=F32)


def _ffn(x, gain, w_in, w_out, *, tm, tf):
    t, d = x.shape
    nf = D_FF // tf
    return pl.pallas_call(
        _ffn_kernel,
        grid=(t // tm, nf),
        in_specs=[
            pl.BlockSpec((tm, d), lambda i, j: (i, 0)),
            pl.BlockSpec((1, d), lambda i, j: (0, 0)),
            pl.BlockSpec((d, tf), lambda i, j: (0, j)),
            pl.BlockSpec((d, tf), lambda i, j: (0, nf + j)),
            pl.BlockSpec((tf, d), lambda i, j: (j, 0)),
        ],
        out_specs=pl.BlockSpec((tm, d), lambda i, j: (i, 0)),
        out_shape=jax.ShapeDtypeStruct((t, d), F32),
        scratch_shapes=[pltpu.VMEM((tm, d), BF16)],
        compiler_params=_compiler_params(("parallel", "arbitrary")),
        name="swiglu_ffn",
    )(x, gain, w_in, w_in, w_out)


def _tiles(batch, seq):
    t = batch * seq
    tm = min(1024, t)
    return dict(
        proj_tm=tm,
        in_proj_tn=IN_PROJ_COLS // 7,
        qkv_tn=512,
        mamba_lt=min(256, seq),
        attn_blk=min(256, seq),
        ffn_tm=min(512, t),
        ffn_tf=D_FF // 2,
    )


def kernel(x, norm_mix, norm_ffn, ssm_w_in, ssm_conv_w, ssm_conv_b, ssm_dt_bias, ssm_a_log, ssm_d,
           ssm_norm_w, ssm_w_out, sb_w_qkv, sb_q_gain, sb_k_gain, sb_w_o, ffn_w_in, ffn_w_out):
    batch, seq, d = x.shape
    t = batch * seq
    tl = _tiles(batch, seq)
    xf = x.reshape(t, d)

    w_in = jnp.pad(ssm_w_in[0], ((0, 0), (0, IN_PROJ_COLS - ssm_w_in.shape[2]))).astype(BF16)
    proj = _norm_proj(xf, norm_mix[0:1], w_in, tm=tl["proj_tm"], tn=tl["in_proj_tn"], out_dtype=F32)
    pad_heads = LANES - SSM_HEADS
    xf = _mamba_core(
        proj, xf, ssm_conv_w[0], ssm_conv_b[0:1],
        jnp.pad(ssm_dt_bias[0:1], ((0, 0), (0, pad_heads))),
        jnp.pad(ssm_a_log[0:1], ((0, 0), (0, pad_heads))),
        jnp.repeat(ssm_d[0:1], SSM_HEAD_DIM, axis=1),
        ssm_norm_w[0:1], ssm_w_out[0].astype(BF16),
        batch=batch, lt=tl["mamba_lt"])
    xf = _ffn(xf, norm_ffn[0:1], ffn_w_in[0].astype(BF16), ffn_w_out[0].astype(BF16),
              tm=tl["ffn_tm"], tf=tl["ffn_tf"])

    head_gain = jnp.concatenate([
        jnp.tile(sb_q_gain[0], SB_HEADS) * (LOG2E * SB_HEAD_DIM ** -0.5),
        jnp.tile(sb_k_gain[0], SB_HEADS),
        jnp.ones((D_MODEL,), F32)])[None, :]
    qkv = _qkv_proj(xf, norm_mix[1:2], sb_w_qkv[0].astype(BF16), head_gain,
                    tm=tl["proj_tm"], tn=tl["qkv_tn"])
    attn = _sb_attention(qkv.reshape(batch, seq, 3 * d), blk=tl["attn_blk"])
    xf = _proj_residual(xf, attn.reshape(t, d), sb_w_o[0].astype(BF16), tm=tl["proj_tm"])
    xf = _ffn(xf, norm_ffn[1:2], ffn_w_in[1].astype(BF16), ffn_w_out[1].astype(BF16),
              tm=tl["ffn_tm"], tf=tl["ffn_tf"])
    return xf.reshape(batch, seq, d)
```

```python
import functools

import jax
import jax.numpy as jnp
from jax import lax
from jax.experimental import pallas as pl
from jax.experimental.pallas import tpu as pltpu

F32 = jnp.float32
BF16 = jnp.bfloat16

D_MODEL = 1024
SSM_HEAD_DIM = 64
SSM_HEADS = 32
SSM_GROUPS = 8
SSM_HEADS_PER_GROUP = SSM_HEADS // SSM_GROUPS
D_STATE = 128
D_INNER = SSM_HEADS * SSM_HEAD_DIM
GROUP_CH = D_INNER // SSM_GROUPS
CONV_WIDTH = 4
SB_HEADS = 16
SB_HEAD_DIM = 64
D_FF = 2816
NORM_EPS = 1e-6
GATED_NORM_EPS = 1e-5
LOG2E = 1.4426950408889634

LANES = 128
SUBLANES = 8
MXU_DIM = 256
VMEM_LIMIT_BYTES = 56 * 1024 * 1024

IN_PROJ_COLS = D_INNER + D_INNER + 2 * SSM_GROUPS * D_STATE


def _rms(x, eps):
    return x * lax.rsqrt(jnp.mean(x * x, axis=-1, keepdims=True) + eps)


def _softplus(x):
    return jnp.maximum(x, 0.0) + jnp.log(1.0 + jnp.exp(-jnp.abs(x)))


def _silu(x):
    return x / (1.0 + jnp.exp(-x))


def _compiler_params(semantics):
    return pltpu.CompilerParams(dimension_semantics=semantics, vmem_limit_bytes=VMEM_LIMIT_BYTES)


def _in_proj_kernel(x_ref, g_ref, w_ref, wdt_ref, o_ref, odt_ref, h_ref):
    @pl.when(pl.program_id(1) == 0)
    def _():
        h = (_rms(x_ref[...], NORM_EPS) * g_ref[...]).astype(BF16)
        h_ref[...] = h
        odt_ref[...] = jnp.dot(h, wdt_ref[...], preferred_element_type=F32)

    o_ref[...] = jnp.dot(h_ref[...], w_ref[...], preferred_element_type=F32)


def _in_proj(x, gain, w, w_dt, *, tm, tn):
    t, d = x.shape
    n = w.shape[1]
    return pl.pallas_call(
        _in_proj_kernel,
        grid=(t // tm, n // tn),
        in_specs=[
            pl.BlockSpec((tm, d), lambda i, j: (i, 0)),
            pl.BlockSpec((1, d), lambda i, j: (0, 0)),
            pl.BlockSpec((d, tn), lambda i, j: (0, j)),
            pl.BlockSpec((d, LANES), lambda i, j: (0, 0)),
        ],
        out_specs=[
            pl.BlockSpec((tm, tn), lambda i, j: (i, j)),
            pl.BlockSpec((tm, LANES), lambda i, j: (i, 0)),
        ],
        out_shape=[jax.ShapeDtypeStruct((t, n), F32), jax.ShapeDtypeStruct((t, LANES), F32)],
        scratch_shapes=[pltpu.VMEM((tm, d), BF16)],
        compiler_params=_compiler_params(("parallel", "arbitrary")),
        name="in_proj",
    )(x, gain, w, w_dt)


def _qkv_proj_kernel(x_ref, g_ref, w_ref, hg_ref, o_ref, h_ref, *, n_norm_tiles):
    j = pl.program_id(1)

    @pl.when(j == 0)
    def _():
        h_ref[...] = (_rms(x_ref[...], NORM_EPS) * g_ref[...]).astype(BF16)

    y = jnp.dot(h_ref[...], w_ref[...], preferred_element_type=F32)

    @pl.when(j < n_norm_tiles)
    def _():
        row = lax.broadcasted_iota(jnp.int32, (MXU_DIM, MXU_DIM), 0) // SB_HEAD_DIM
        col = lax.broadcasted_iota(jnp.int32, (MXU_DIM, MXU_DIM), 1) // SB_HEAD_DIM
        head_mean = jnp.where(row == col, 1.0 / SB_HEAD_DIM, 0.0).astype(BF16)
        y2 = (y * y).astype(BF16)
        ms = jnp.concatenate(
            [jnp.dot(y2[:, c:c + MXU_DIM], head_mean, preferred_element_type=F32)
             for c in range(0, y.shape[1], MXU_DIM)], axis=1)
        o_ref[...] = (y * lax.rsqrt(ms + NORM_EPS) * hg_ref[...]).astype(o_ref.dtype)

    @pl.when(j >= n_norm_tiles)
    def _():
        o_ref[...] = y.astype(o_ref.dtype)


def _qkv_proj(x, gain, w, head_gain, *, tm, tn):
    t, d = x.shape
    n = w.shape[1]
    return pl.pallas_call(
        functools.partial(_qkv_proj_kernel, n_norm_tiles=(2 * D_MODEL) // tn),
        grid=(t // tm, n // tn),
        in_specs=[
            pl.BlockSpec((tm, d), lambda i, j: (i, 0)),
            pl.BlockSpec((1, d), lambda i, j: (0, 0)),
            pl.BlockSpec((d, tn), lambda i, j: (0, j)),
            pl.BlockSpec((1, tn), lambda i, j: (0, j)),
        ],
        out_specs=pl.BlockSpec((tm, tn), lambda i, j: (i, j)),
        out_shape=jax.ShapeDtypeStruct((t, n), BF16),
        scratch_shapes=[pltpu.VMEM((tm, d), BF16)],
        compiler_params=_compiler_params(("parallel", "arbitrary")),
        name="qkv_proj",
    )(x, gain, w, head_gain)


def _conv_silu(u_ref, tail_ref, buf_ref, w_ref, b_ref, g, k_local, width, lt):
    lanes = slice(k_local * width, (k_local + 1) * width)
    buf = buf_ref.at[k_local]
    buf[0:SUBLANES, :] = tail_ref[g]
    buf[SUBLANES:SUBLANES + lt, :] = u_ref[:, lanes]
    tail_ref[g] = buf[lt:lt + SUBLANES, :]
    acc = b_ref[:, lanes]
    for k in range(CONV_WIDTH):
        start = SUBLANES - (CONV_WIDTH - 1) + k
        acc = acc + w_ref[k:k + 1, lanes] * buf[start:start + lt, :]
    return _silu(acc)


def _head_select(pieces):
    lane = lax.broadcasted_iota(jnp.int32, (1, LANES), 1)
    lo = lane < SSM_HEAD_DIM
    return jnp.concatenate(
        [jnp.where(lo, pieces[0], pieces[1]), jnp.where(lo, pieces[2], pieces[3])], axis=1)


def _split3_dot(tri_bf16, a):
    a0 = a.astype(BF16)
    r1 = a - a0.astype(F32)
    a1 = r1.astype(BF16)
    a2 = (r1 - a1.astype(F32)).astype(BF16)
    out = jnp.dot(tri_bf16, a0, preferred_element_type=F32)
    out = out + jnp.dot(tri_bf16, a1, preferred_element_type=F32)
    return out + jnp.dot(tri_bf16, a2, preferred_element_type=F32)


def _mamba_core_kernel(z_ref, ux_ref, ub_ref, uc_ref, dt_ref, xres_ref,
                       cwx_ref, cwb_ref, cwc_ref, cbx_ref, cbb_ref, cbc_ref,
                       dtb_ref, alog_ref, dch_ref, nw_ref, wout_ref,
                       o_ref,
                       state_ref, tailx_ref, tailb_ref, tailc_ref,
                       bufx_ref, bufb_ref, bufc_ref, dts_ref, acs_ref, acst_ref, *, lt, lc, gps):
    i = pl.program_id(1)
    gstep = pl.program_id(2)

    @pl.when(i == 0)
    def _():
        for k_local in range(gps):
            g = gstep * gps + k_local
            state_ref[g] = jnp.zeros(state_ref.shape[1:], F32)
            tailx_ref[g] = jnp.zeros(tailx_ref.shape[1:], F32)
            tailb_ref[g] = jnp.zeros(tailb_ref.shape[1:], F32)
            tailc_ref[g] = jnp.zeros(tailc_ref.shape[1:], F32)

    @pl.when(gstep == 0)
    def _():
        dt_all = _softplus(dt_ref[...] + dtb_ref[...])
        a = dt_all * (-jnp.exp(alog_ref[...]))
        row = lax.broadcasted_iota(jnp.int32, (lt, lt), 0)
        col = lax.broadcasted_iota(jnp.int32, (lt, lt), 1)
        same_chunk_causal = (row >= col) & ((row // lc) == (col // lc))
        acs_all = _split3_dot(same_chunk_causal.astype(BF16), a)
        dts_ref[...] = dt_all
        acs_ref[...] = acs_all
        acst_ref[...] = acs_all.T

    dt = dts_ref[...]
    acs = acs_ref[...]
    lane = lax.broadcasted_iota(jnp.int32, (1, LANES), 1)
    row_c = lax.broadcasted_iota(jnp.int32, (lc, lc), 0)
    col_c = lax.broadcasted_iota(jnp.int32, (lc, lc), 1)
    causal = row_c >= col_c
    second_head = (lane // SSM_HEAD_DIM) == 1
    normed = [_mamba_group(k_local, gstep * gps + k_local, dt, acs, lane, causal, second_head,
                           z_ref, ux_ref, ub_ref, uc_ref, cwx_ref, cwb_ref, cwc_ref,
                           cbx_ref, cbb_ref, cbc_ref, dch_ref, nw_ref,
                           state_ref, tailx_ref, tailb_ref, tailc_ref,
                           bufx_ref, bufb_ref, bufc_ref, acst_ref, lt, lc)
              for k_local in range(gps)]
    contrib = jnp.dot(jnp.concatenate(normed, axis=1), wout_ref[...], preferred_element_type=F32)

    @pl.when(gstep == 0)
    def _():
        o_ref[...] = xres_ref[...] + contrib

    @pl.when(gstep != 0)
    def _():
        o_ref[...] += contrib


def _mamba_group(k_local, g, dt, acs, lane, causal, second_head,
                 z_ref, ux_ref, ub_ref, uc_ref, cwx_ref, cwb_ref, cwc_ref,
                 cbx_ref, cbb_ref, cbc_ref, dch_ref, nw_ref,
                 state_ref, tailx_ref, tailb_ref, tailc_ref,
                 bufx_ref, bufb_ref, bufc_ref, acst_ref, lt, lc):
    ch = slice(k_local * GROUP_CH, (k_local + 1) * GROUP_CH)
    xs = _conv_silu(ux_ref, tailx_ref, bufx_ref, cwx_ref, cbx_ref, g, k_local, GROUP_CH, lt)
    bm = _conv_silu(ub_ref, tailb_ref, bufb_ref, cwb_ref, cbb_ref, g, k_local, D_STATE, lt)
    cm = _conv_silu(uc_ref, tailc_ref, bufc_ref, cwc_ref, cbc_ref, g, k_local, D_STATE, lt)
    bm16 = bm.astype(BF16)
    cm16 = cm.astype(BF16)

    dt_cols, acs_cols, acs_rows = [], [], []
    for r in range(SSM_HEADS_PER_GROUP):
        h = g * SSM_HEADS_PER_GROUP + r
        sel = lane == h
        dt_cols.append(jnp.sum(jnp.where(sel, dt, 0.0), axis=1, keepdims=True))
        acs_cols.append(jnp.sum(jnp.where(sel, acs, 0.0), axis=1, keepdims=True))
        acs_rows.append(acst_ref[pl.ds(h, 1), :])

    dt_x = _head_select(dt_cols)
    acs_x = _head_select(acs_cols)
    xdt = xs * dt_x
    decay_in = jnp.exp(acs_x)

    state = state_ref[g]
    y_chunks = []
    for c in range(lt // lc):
        lo, hi = c * lc, (c + 1) * lc
        cb = lax.dot_general(cm16[lo:hi], bm16[lo:hi], (((1,), (1,)), ((), ())),
                             preferred_element_type=F32)
        y_tiles = [None, None]
        for r in range(SSM_HEADS_PER_GROUP):
            seg = acs_cols[r][lo:hi] - acs_rows[r][:, lo:hi]
            lmat = jnp.exp(jnp.where(causal, seg, -jnp.inf))
            gm = (cb * lmat).astype(BF16)
            tile = r // 2
            x_tile = xdt[lo:hi, tile * LANES:(tile + 1) * LANES]
            keep = second_head if r % 2 else jnp.logical_not(second_head)
            part = jnp.dot(gm, jnp.where(keep, x_tile, 0.0).astype(BF16), preferred_element_type=F32)
            y_tiles[tile] = part if y_tiles[tile] is None else y_tiles[tile] + part
        y_inter = jnp.dot(cm16[lo:hi], state.astype(BF16), preferred_element_type=F32)
        y_chunks.append(jnp.concatenate(y_tiles, axis=1) + y_inter * decay_in[lo:hi])
        last_x = acs_x[hi - 1:hi, :]
        w_end = (xdt[lo:hi] * jnp.exp(last_x - acs_x[lo:hi])).astype(BF16)
        upd = lax.dot_general(bm16[lo:hi], w_end, (((0,), (0,)), ((), ())),
                              preferred_element_type=F32)
        state = state * jnp.exp(last_x) + upd
    state_ref[g] = state
    y = jnp.concatenate(y_chunks, axis=0) + dch_ref[:, ch] * xs

    yg = y * _silu(z_ref[:, ch])
    return (_rms(yg, GATED_NORM_EPS) * nw_ref[:, ch]).astype(BF16)


def _mamba_core(proj, dt_raw, x_res, conv_w, conv_b, dt_bias, a_log, d_ch, norm_w, w_out, *,
                batch, lt, lc, gps):
    t = proj.shape[0]
    nt = t // batch // lt
    xw = gps * GROUP_CH
    sw = gps * D_STATE

    def rows(b, i, s):
        return b * nt + i

    x_blk = D_INNER // xw
    b_blk = 2 * D_INNER // sw
    c_blk = b_blk + SSM_GROUPS * D_STATE // sw
    cw_b_blk = D_INNER // sw
    cw_c_blk = cw_b_blk + SSM_GROUPS * D_STATE // sw
    in_specs = [
        pl.BlockSpec((lt, xw), lambda b, i, s: (rows(b, i, s), s)),
        pl.BlockSpec((lt, xw), lambda b, i, s: (rows(b, i, s), x_blk + s)),
        pl.BlockSpec((lt, sw), lambda b, i, s: (rows(b, i, s), b_blk + s)),
        pl.BlockSpec((lt, sw), lambda b, i, s: (rows(b, i, s), c_blk + s)),
        pl.BlockSpec((lt, LANES), lambda b, i, s: (rows(b, i, s), 0)),
        pl.BlockSpec((lt, D_MODEL), lambda b, i, s: (rows(b, i, s), 0)),
        pl.BlockSpec((CONV_WIDTH, xw), lambda b, i, s: (0, s)),
        pl.BlockSpec((CONV_WIDTH, sw), lambda b, i, s: (0, cw_b_blk + s)),
        pl.BlockSpec((CONV_WIDTH, sw), lambda b, i, s: (0, cw_c_blk + s)),
        pl.BlockSpec((1, xw), lambda b, i, s: (0, s)),
        pl.BlockSpec((1, sw), lambda b, i, s: (0, cw_b_blk + s)),
        pl.BlockSpec((1, sw), lambda b, i, s: (0, cw_c_blk + s)),
        pl.BlockSpec((1, LANES), lambda b, i, s: (0, 0)),
        pl.BlockSpec((1, LANES), lambda b, i, s: (0, 0)),
        pl.BlockSpec((1, xw), lambda b, i, s: (0, s)),
        pl.BlockSpec((1, xw), lambda b, i, s: (0, s)),
        pl.BlockSpec((xw, D_MODEL), lambda b, i, s: (s, 0)),
    ]
    return pl.pallas_call(
        functools.partial(_mamba_core_kernel, lt=lt, lc=lc, gps=gps),
        grid=(batch, nt, SSM_GROUPS // gps),
        in_specs=in_specs,
        out_specs=pl.BlockSpec((lt, D_MODEL), lambda b, i, s: (rows(b, i, s), 0)),
        out_shape=jax.ShapeDtypeStruct((t, D_MODEL), F32),
        scratch_shapes=[
            pltpu.VMEM((SSM_GROUPS, D_STATE, GROUP_CH), F32),
            pltpu.VMEM((SSM_GROUPS, SUBLANES, GROUP_CH), F32),
            pltpu.VMEM((SSM_GROUPS, SUBLANES, D_STATE), F32),
            pltpu.VMEM((SSM_GROUPS, SUBLANES, D_STATE), F32),
            pltpu.VMEM((gps, lt + SUBLANES, GROUP_CH), F32),
            pltpu.VMEM((gps, lt + SUBLANES, D_STATE), F32),
            pltpu.VMEM((gps, lt + SUBLANES, D_STATE), F32),
            pltpu.VMEM((lt, LANES), F32),
            pltpu.VMEM((lt, LANES), F32),
            pltpu.VMEM((LANES, lt), F32),
        ],
        compiler_params=_compiler_params(("parallel", "arbitrary", "arbitrary")),
        name="mamba_core",
    )(proj, proj, proj, proj, dt_raw, x_res,
      conv_w, conv_w, conv_w, conv_b, conv_b, conv_b,
      dt_bias, a_log, d_ch, norm_w, w_out)


def _softplus2(z2):
    neg_abs = lax.bitcast_convert_type(
        lax.bitcast_convert_type(z2, jnp.uint32) | jnp.uint32(0x80000000), F32)
    return jnp.maximum(z2, 0.0) + jnp.log(1.0 + jnp.exp2(neg_abs)) * LOG2E


def _sb_attn_kernel(q_ref, k_ref, v_ref, o_ref, qs_ref, carry_ref, acc_ref, zs_ref, *, blk):
    qi = pl.program_id(2)
    q = q_ref[0]
    lane = lax.broadcasted_iota(jnp.int32, (1, LANES), 1)
    first_head = lane < SB_HEAD_DIM
    zero = jnp.zeros_like(q)
    q0 = jnp.where(first_head, q, zero)
    q1 = jnp.where(first_head, zero, q)
    qs_ref[...] = jnp.concatenate([q0[:blk], q1[:blk], q0[blk:], q1[blk:]], axis=0)

    row = lax.broadcasted_iota(jnp.int32, (blk, blk), 0)
    col = lax.broadcasted_iota(jnp.int32, (blk, blk), 1)
    neg_suffix = jnp.where(row >= col, -1.0, 0.0).astype(BF16)
    row2 = lax.broadcasted_iota(jnp.int32, (2 * blk, blk), 0) & (blk - 1)
    col2 = lax.broadcasted_iota(jnp.int32, (2 * blk, blk), 1)
    diag_causal = col2 < row2

    def logits(q_rows, j):
        kj = k_ref[0, pl.ds(pl.multiple_of(j * blk, blk), blk), :]
        return lax.dot_general(q_rows, kj, (((1,), (1,)), ((), ())), preferred_element_type=F32)

    def weigh(z, j, carry, mask):
        vj = v_ref[0, pl.ds(pl.multiple_of(j * blk, blk), blk), :]
        sp = _softplus2(z)
        if mask is not None:
            sp = jnp.where(mask, sp, 0.0)
        incl = jnp.dot(sp.astype(BF16), neg_suffix, preferred_element_type=F32)
        w = jnp.exp2(z + incl + carry)
        if mask is not None:
            w = jnp.where(mask, w, 0.0)
        w = w.astype(BF16)
        lhs = jnp.concatenate(
            [jnp.concatenate([w[r:r + blk], w[r + blk:r + 2 * blk]], axis=1)
             for r in range(0, z.shape[0], 2 * blk)], axis=0)
        v_heads = jnp.concatenate([jnp.where(first_head, vj, jnp.zeros_like(vj)),
                                   jnp.where(first_head, jnp.zeros_like(vj), vj)], axis=0)
        return carry + incl[:, 0:1], jnp.dot(lhs, v_heads, preferred_element_type=F32)

    top = qs_ref[0:2 * blk, :]
    bottom = qs_ref[2 * blk:4 * blk, :]
    no_carry = jnp.zeros((2 * blk, 1), F32)
    jd = 2 * qi
    c_bot, a_bot = weigh(logits(bottom, jd + 1), jd + 1, no_carry, diag_causal)
    c_bot, a_bot2 = weigh(logits(bottom, jd), jd, c_bot, None)
    c_top, a_top = weigh(logits(top, jd), jd, no_carry, diag_causal)
    carry_ref[0:2 * blk, :] = c_top
    carry_ref[2 * blk:4 * blk, :] = c_bot
    acc_ref[0:blk, :] = a_top
    acc_ref[blk:2 * blk, :] = a_bot + a_bot2

    zs_ref[...] = logits(qs_ref[...], jnp.maximum(jd - 1, 0))

    def make_body(n_blocks, first_block):
        def body(t, _):
            j = first_block - n_blocks * t
            z = zs_ref[...]
            c = carry_ref[...]
            total = None
            for u in range(n_blocks):
                if u + 1 < n_blocks:
                    z_next = logits(qs_ref[...], j - u - 1)
                else:
                    zs_ref[...] = logits(qs_ref[...], jnp.maximum(j - n_blocks, 0))
                c, a = weigh(z, j - u, c, None)
                total = a if total is None else total + a
                z = z_next
            carry_ref[...] = c
            acc_ref[...] += total
            return 0
        return body

    n_quads = lax.shift_right_logical(qi, 1)
    lax.fori_loop(0, n_quads, make_body(4, jd - 1), 0)
    lax.fori_loop(0, qi & 1, make_body(2, jd - 1 - 4 * n_quads), 0)
    o_ref[0] = acc_ref[...].astype(o_ref.dtype)


def _sb_attention(qkv, *, blk):
    b, l, _ = qkv.shape
    pairs = SB_HEADS * SB_HEAD_DIM // LANES
    return pl.pallas_call(
        functools.partial(_sb_attn_kernel, blk=blk),
        grid=(b, pairs, l // (2 * blk)),
        in_specs=[
            pl.BlockSpec((1, 2 * blk, LANES), lambda bi, p, i: (bi, i, p)),
            pl.BlockSpec((1, l, LANES), lambda bi, p, i: (bi, 0, pairs + p)),
            pl.BlockSpec((1, l, LANES), lambda bi, p, i: (bi, 0, 2 * pairs + p)),
        ],
        out_specs=pl.BlockSpec((1, 2 * blk, LANES), lambda bi, p, i: (bi, i, p)),
        out_shape=jax.ShapeDtypeStruct((b, l, D_MODEL), BF16),
        scratch_shapes=[
            pltpu.VMEM((4 * blk, LANES), BF16),
            pltpu.VMEM((4 * blk, 1), F32),
            pltpu.VMEM((2 * blk, LANES), F32),
            pltpu.VMEM((4 * blk, blk), F32),
        ],
        compiler_params=_compiler_params(("parallel", "parallel", "arbitrary")),
        name="sb_attention",
    )(qkv, qkv, qkv)


def _proj_residual_kernel(x_ref, a_ref, w_ref, o_ref):
    o_ref[...] = x_ref[...] + jnp.dot(a_ref[...], w_ref[...], preferred_element_type=F32)


def _proj_residual(x, a, w, *, tm):
    t, d = x.shape
    k = a.shape[1]
    return pl.pallas_call(
        _proj_residual_kernel,
        grid=(t // tm,),
        in_specs=[
            pl.BlockSpec((tm, d), lambda i: (i, 0)),
            pl.BlockSpec((tm, k), lambda i: (i, 0)),
            pl.BlockSpec((k, d), lambda i: (0, 0)),
        ],
        out_specs=pl.BlockSpec((tm, d), lambda i: (i, 0)),
        out_shape=jax.ShapeDtypeStruct((t, d), F32),
        compiler_params=_compiler_params(("parallel",)),
        name="proj_residual",
    )(x, a, w)


def _ffn_kernel(x_ref, g_ref, wg_ref, wu_ref, wo_ref, o_ref, h_ref):
    j = pl.program_id(1)

    @pl.when(j == 0)
    def _():
        x = x_ref[...]
        h_ref[...] = (_rms(x, NORM_EPS) * g_ref[...]).astype(BF16)
        o_ref[...] = x

    h = h_ref[...]
    gate = jnp.dot(h, wg_ref[...], preferred_element_type=F32)
    up = jnp.dot(h, wu_ref[...], preferred_element_type=F32)
    act = (_silu(gate) * up).astype(BF16)
    o_ref[...] += jnp.dot(act, wo_ref[...], preferred_element_type=F32)


def _ffn(x, gain, w_in, w_out, *, tm, tf):
    t, d = x.shape
    nf = D_FF // tf
    weights_mode = pl.Buffered(1) if nf == 1 else None
    return pl.pallas_call(
        _ffn_kernel,
        grid=(t // tm, nf),
        in_specs=[
            pl.BlockSpec((tm, d), lambda i, j: (i, 0)),
            pl.BlockSpec((1, d), lambda i, j: (0, 0)),
            pl.BlockSpec((d, tf), lambda i, j: (0, j), pipeline_mode=weights_mode),
            pl.BlockSpec((d, tf), lambda i, j: (0, nf + j), pipeline_mode=weights_mode),
            pl.BlockSpec((tf, d), lambda i, j: (j, 0), pipeline_mode=weights_mode),
        ],
        out_specs=pl.BlockSpec((tm, d), lambda i, j: (i, 0)),
        out_shape=jax.ShapeDtypeStruct((t, d), F32),
        scratch_shapes=[pltpu.VMEM((tm, d), BF16)],
        compiler_params=_compiler_params(("parallel", "arbitrary")),
        name="swiglu_ffn",
    )(x, gain, w_in, w_in, w_out)


def _tiles(batch, seq):
    t = batch * seq
    tm = min(1024, t)
    return dict(
        proj_tm=tm,
        in_proj_tn=1024,
        qkv_tn=D_MODEL,
        mamba_lt=min(256, seq),
        mamba_lc=min(128, seq),
        mamba_gps=4,
        attn_blk=min(256, seq),
        ffn_tm=min(512, t),
        ffn_tf=D_FF,
    )


def kernel(x, norm_mix, norm_ffn, ssm_w_in, ssm_conv_w, ssm_conv_b, ssm_dt_bias, ssm_a_log, ssm_d,
           ssm_norm_w, ssm_w_out, sb_w_qkv, sb_q_gain, sb_k_gain, sb_w_o, ffn_w_in, ffn_w_out):
    batch, seq, d = x.shape
    t = batch * seq
    tl = _tiles(batch, seq)
    xf = x.reshape(t, d)

    pad_heads = LANES - SSM_HEADS
    w_in = ssm_w_in[0].astype(BF16)
    proj, dt_raw = _in_proj(xf, norm_mix[0:1], w_in[:, :IN_PROJ_COLS],
                            jnp.pad(w_in[:, IN_PROJ_COLS:], ((0, 0), (0, pad_heads))),
                            tm=tl["proj_tm"], tn=tl["in_proj_tn"])
    xf = _mamba_core(
        proj, dt_raw, xf, ssm_conv_w[0], ssm_conv_b[0:1],
        jnp.pad(ssm_dt_bias[0:1], ((0, 0), (0, pad_heads))),
        jnp.pad(ssm_a_log[0:1], ((0, 0), (0, pad_heads))),
        jnp.repeat(ssm_d[0:1], SSM_HEAD_DIM, axis=1),
        ssm_norm_w[0:1], ssm_w_out[0].astype(BF16),
        batch=batch, lt=tl["mamba_lt"], lc=tl["mamba_lc"], gps=tl["mamba_gps"])
    xf = _ffn(xf, norm_ffn[0:1], ffn_w_in[0].astype(BF16), ffn_w_out[0].astype(BF16),
              tm=tl["ffn_tm"], tf=tl["ffn_tf"])

    head_gain = jnp.concatenate([
        jnp.tile(sb_q_gain[0], SB_HEADS) * (LOG2E * SB_HEAD_DIM ** -0.5),
        jnp.tile(sb_k_gain[0], SB_HEADS),
        jnp.ones((D_MODEL,), F32)])[None, :]
    qkv = _qkv_proj(xf, norm_mix[1:2], sb_w_qkv[0].astype(BF16), head_gain,
                    tm=tl["proj_tm"], tn=tl["qkv_tn"])
    attn = _sb_attention(qkv.reshape(batch, seq, 3 * d), blk=tl["attn_blk"])
    xf = _proj_residual(xf, attn.reshape(t, d), sb_w_o[0].astype(BF16), tm=tl["proj_tm"])
    xf = _ffn(xf, norm_ffn[1:2], ffn_w_in[1].astype(BF16), ffn_w_out[1].astype(BF16),
              tm=tl["ffn_tm"], tf=tl["ffn_tf"])
    return xf.reshape(batch, seq, d)
```

```python
import functools

import jax
import jax.numpy as jnp
from jax import lax
from jax.experimental import pallas as pl
from jax.experimental.pallas import tpu as pltpu

F32 = jnp.float32
BF16 = jnp.bfloat16

D_MODEL = 1024
SSM_HEAD_DIM = 64
SSM_HEADS = 32
SSM_GROUPS = 8
SSM_HEADS_PER_GROUP = SSM_HEADS // SSM_GROUPS
D_STATE = 128
D_INNER = SSM_HEADS * SSM_HEAD_DIM
GROUP_CH = D_INNER // SSM_GROUPS
CONV_WIDTH = 4
SB_HEADS = 16
SB_HEAD_DIM = 64
D_FF = 2816
NORM_EPS = 1e-6
GATED_NORM_EPS = 1e-5
LOG2E = 1.4426950408889634
UNDERFLOW_LOG2 = 1100.0

LANES = 128
SUBLANES = 8
MXU_DIM = 256
VMEM_LIMIT_BYTES = 56 * 1024 * 1024

IN_PROJ_COLS = D_INNER + D_INNER + 2 * SSM_GROUPS * D_STATE


def _rms(x, eps):
    return x * lax.rsqrt(jnp.mean(x * x, axis=-1, keepdims=True) + eps)


def _softplus(x):
    return jnp.maximum(x, 0.0) + jnp.log(1.0 + jnp.exp(-jnp.abs(x)))


def _silu(x):
    return x / (1.0 + jnp.exp(-x))


def _compiler_params(semantics):
    return pltpu.CompilerParams(dimension_semantics=semantics, vmem_limit_bytes=VMEM_LIMIT_BYTES)


def _in_proj_kernel(x_ref, g_ref, w_ref, wdt_ref, o_ref, odt_ref, h_ref):
    @pl.when(pl.program_id(1) == 0)
    def _():
        h = (_rms(x_ref[...], NORM_EPS) * g_ref[...]).astype(BF16)
        h_ref[...] = h
        odt_ref[...] = jnp.dot(h, wdt_ref[...], preferred_element_type=F32)

    o_ref[...] = jnp.dot(h_ref[...], w_ref[...], preferred_element_type=F32)


def _in_proj(x, gain, w, w_dt, *, tm, tn):
    t, d = x.shape
    n = w.shape[1]
    return pl.pallas_call(
        _in_proj_kernel,
        grid=(t // tm, n // tn),
        in_specs=[
            pl.BlockSpec((tm, d), lambda i, j: (i, 0)),
            pl.BlockSpec((1, d), lambda i, j: (0, 0)),
            pl.BlockSpec((d, tn), lambda i, j: (0, j)),
            pl.BlockSpec((d, LANES), lambda i, j: (0, 0)),
        ],
        out_specs=[
            pl.BlockSpec((tm, tn), lambda i, j: (i, j)),
            pl.BlockSpec((tm, LANES), lambda i, j: (i, 0)),
        ],
        out_shape=[jax.ShapeDtypeStruct((t, n), F32), jax.ShapeDtypeStruct((t, LANES), F32)],
        scratch_shapes=[pltpu.VMEM((tm, d), BF16)],
        compiler_params=_compiler_params(("parallel", "arbitrary")),
        name="in_proj",
    )(x, gain, w, w_dt)


def _qkv_proj_kernel(x_ref, g_ref, w_ref, hg_ref, o_ref, h_ref, *, n_norm_tiles):
    j = pl.program_id(1)

    @pl.when(j == 0)
    def _():
        h_ref[...] = (_rms(x_ref[...], NORM_EPS) * g_ref[...]).astype(BF16)

    y = jnp.dot(h_ref[...], w_ref[...], preferred_element_type=F32)

    @pl.when(j < n_norm_tiles)
    def _():
        row = lax.broadcasted_iota(jnp.int32, (MXU_DIM, MXU_DIM), 0) // SB_HEAD_DIM
        col = lax.broadcasted_iota(jnp.int32, (MXU_DIM, MXU_DIM), 1) // SB_HEAD_DIM
        head_mean = jnp.where(row == col, 1.0 / SB_HEAD_DIM, 0.0).astype(BF16)
        y2 = (y * y).astype(BF16)
        ms = jnp.concatenate(
            [jnp.dot(y2[:, c:c + MXU_DIM], head_mean, preferred_element_type=F32)
             for c in range(0, y.shape[1], MXU_DIM)], axis=1)
        o_ref[...] = (y * lax.rsqrt(ms + NORM_EPS) * hg_ref[...]).astype(o_ref.dtype)

    @pl.when(j >= n_norm_tiles)
    def _():
        o_ref[...] = y.astype(o_ref.dtype)


def _qkv_proj(x, gain, w, head_gain, *, tm, tn):
    t, d = x.shape
    n = w.shape[1]
    return pl.pallas_call(
        functools.partial(_qkv_proj_kernel, n_norm_tiles=(2 * D_MODEL) // tn),
        grid=(t // tm, n // tn),
        in_specs=[
            pl.BlockSpec((tm, d), lambda i, j: (i, 0)),
            pl.BlockSpec((1, d), lambda i, j: (0, 0)),
            pl.BlockSpec((d, tn), lambda i, j: (0, j)),
            pl.BlockSpec((1, tn), lambda i, j: (0, j)),
        ],
        out_specs=pl.BlockSpec((tm, tn), lambda i, j: (i, j)),
        out_shape=jax.ShapeDtypeStruct((t, n), BF16),
        scratch_shapes=[pltpu.VMEM((tm, d), BF16)],
        compiler_params=_compiler_params(("parallel", "arbitrary")),
        name="qkv_proj",
    )(x, gain, w, head_gain)


def _conv_silu(u_ref, tail_ref, buf_ref, w_ref, b_ref, g, k_local, width, lt):
    lanes = slice(k_local * width, (k_local + 1) * width)
    buf = buf_ref.at[k_local]
    buf[0:SUBLANES, :] = tail_ref[g]
    buf[SUBLANES:SUBLANES + lt, :] = u_ref[:, lanes]
    tail_ref[g] = buf[lt:lt + SUBLANES, :]
    acc = b_ref[:, lanes]
    for k in range(CONV_WIDTH):
        start = SUBLANES - (CONV_WIDTH - 1) + k
        acc = acc + w_ref[k:k + 1, lanes] * buf[start:start + lt, :]
    return _silu(acc)


def _head_select(pieces):
    lane = lax.broadcasted_iota(jnp.int32, (1, LANES), 1)
    lo = lane < SSM_HEAD_DIM
    return jnp.concatenate(
        [jnp.where(lo, pieces[0], pieces[1]), jnp.where(lo, pieces[2], pieces[3])], axis=1)


def _split3_dot(tri_bf16, a):
    a0 = a.astype(BF16)
    r1 = a - a0.astype(F32)
    a1 = r1.astype(BF16)
    a2 = (r1 - a1.astype(F32)).astype(BF16)
    out = jnp.dot(tri_bf16, a0, preferred_element_type=F32)
    out = out + jnp.dot(tri_bf16, a1, preferred_element_type=F32)
    return out + jnp.dot(tri_bf16, a2, preferred_element_type=F32)


def _mamba_core_kernel(z_ref, ux_ref, ub_ref, uc_ref, dt_ref, xres_ref,
                       cwx_ref, cwb_ref, cwc_ref, cbx_ref, cbb_ref, cbc_ref,
                       dtb_ref, alog_ref, dch_ref, nw_ref, wout_ref,
                       o_ref,
                       state_ref, tailx_ref, tailb_ref, tailc_ref,
                       bufx_ref, bufb_ref, bufc_ref, dts_ref, acs_ref, acst_ref, *, lt, lc, gps):
    i = pl.program_id(1)
    gstep = pl.program_id(2)

    @pl.when(i == 0)
    def _():
        for k_local in range(gps):
            g = gstep * gps + k_local
            state_ref[g] = jnp.zeros(state_ref.shape[1:], F32)
            tailx_ref[g] = jnp.zeros(tailx_ref.shape[1:], F32)
            tailb_ref[g] = jnp.zeros(tailb_ref.shape[1:], F32)
            tailc_ref[g] = jnp.zeros(tailc_ref.shape[1:], F32)

    @pl.when(gstep == 0)
    def _():
        dt_all = _softplus(dt_ref[...] + dtb_ref[...])
        a = dt_all * (-jnp.exp(alog_ref[...]))
        row = lax.broadcasted_iota(jnp.int32, (lt, lt), 0)
        col = lax.broadcasted_iota(jnp.int32, (lt, lt), 1)
        same_chunk_causal = (row >= col) & ((row // lc) == (col // lc))
        acs_all = _split3_dot(same_chunk_causal.astype(BF16), a)
        dts_ref[...] = dt_all
        acs_ref[...] = acs_all
        acst_ref[...] = acs_all.T

    dt = dts_ref[...]
    acs = acs_ref[...]
    lane = lax.broadcasted_iota(jnp.int32, (1, LANES), 1)
    row_c = lax.broadcasted_iota(jnp.int32, (lc, lc), 0)
    col_c = lax.broadcasted_iota(jnp.int32, (lc, lc), 1)
    causal = row_c >= col_c
    second_head = (lane // SSM_HEAD_DIM) == 1
    normed = [_mamba_group(k_local, gstep * gps + k_local, dt, acs, lane, causal, second_head,
                           z_ref, ux_ref, ub_ref, uc_ref, cwx_ref, cwb_ref, cwc_ref,
                           cbx_ref, cbb_ref, cbc_ref, dch_ref, nw_ref,
                           state_ref, tailx_ref, tailb_ref, tailc_ref,
                           bufx_ref, bufb_ref, bufc_ref, acst_ref, lt, lc)
              for k_local in range(gps)]
    contrib = jnp.dot(jnp.concatenate(normed, axis=1), wout_ref[...], preferred_element_type=F32)

    @pl.when(gstep == 0)
    def _():
        o_ref[...] = xres_ref[...] + contrib

    @pl.when(gstep != 0)
    def _():
        o_ref[...] += contrib


def _mamba_group(k_local, g, dt, acs, lane, causal, second_head,
                 z_ref, ux_ref, ub_ref, uc_ref, cwx_ref, cwb_ref, cwc_ref,
                 cbx_ref, cbb_ref, cbc_ref, dch_ref, nw_ref,
                 state_ref, tailx_ref, tailb_ref, tailc_ref,
                 bufx_ref, bufb_ref, bufc_ref, acst_ref, lt, lc):
    ch = slice(k_local * GROUP_CH, (k_local + 1) * GROUP_CH)
    xs = _conv_silu(ux_ref, tailx_ref, bufx_ref, cwx_ref, cbx_ref, g, k_local, GROUP_CH, lt)
    bm = _conv_silu(ub_ref, tailb_ref, bufb_ref, cwb_ref, cbb_ref, g, k_local, D_STATE, lt)
    cm = _conv_silu(uc_ref, tailc_ref, bufc_ref, cwc_ref, cbc_ref, g, k_local, D_STATE, lt)
    bm16 = bm.astype(BF16)
    cm16 = cm.astype(BF16)

    dt_cols, acs_cols, acs_rows = [], [], []
    for r in range(SSM_HEADS_PER_GROUP):
        h = g * SSM_HEADS_PER_GROUP + r
        sel = lane == h
        dt_cols.append(jnp.sum(jnp.where(sel, dt, 0.0), axis=1, keepdims=True))
        acs_cols.append(jnp.sum(jnp.where(sel, acs, 0.0), axis=1, keepdims=True))
        acs_rows.append(acst_ref[pl.ds(h, 1), :])

    dt_x = _head_select(dt_cols)
    acs_x = _head_select(acs_cols)
    xdt = xs * dt_x
    decay_in = jnp.exp(acs_x)

    state = state_ref[g]
    y_chunks = []
    for c in range(lt // lc):
        lo, hi = c * lc, (c + 1) * lc
        cb = lax.dot_general(cm16[lo:hi], bm16[lo:hi], (((1,), (1,)), ((), ())),
                             preferred_element_type=F32)
        y_tiles = [None, None]
        for r in range(SSM_HEADS_PER_GROUP):
            seg = acs_cols[r][lo:hi] - acs_rows[r][:, lo:hi]
            lmat = jnp.exp(jnp.where(causal, seg, -jnp.inf))
            gm = (cb * lmat).astype(BF16)
            tile = r // 2
            x_tile = xdt[lo:hi, tile * LANES:(tile + 1) * LANES]
            keep = second_head if r % 2 else jnp.logical_not(second_head)
            part = jnp.dot(gm, jnp.where(keep, x_tile, 0.0).astype(BF16), preferred_element_type=F32)
            y_tiles[tile] = part if y_tiles[tile] is None else y_tiles[tile] + part
        y_inter = jnp.dot(cm16[lo:hi], state.astype(BF16), preferred_element_type=F32)
        y_chunks.append(jnp.concatenate(y_tiles, axis=1) + y_inter * decay_in[lo:hi])
        last_x = acs_x[hi - 1:hi, :]
        w_end = (xdt[lo:hi] * jnp.exp(last_x - acs_x[lo:hi])).astype(BF16)
        upd = lax.dot_general(bm16[lo:hi], w_end, (((0,), (0,)), ((), ())),
                              preferred_element_type=F32)
        state = state * jnp.exp(last_x) + upd
    state_ref[g] = state
    y = jnp.concatenate(y_chunks, axis=0) + dch_ref[:, ch] * xs

    yg = y * _silu(z_ref[:, ch])
    return (_rms(yg, GATED_NORM_EPS) * nw_ref[:, ch]).astype(BF16)


def _mamba_core(proj, dt_raw, x_res, conv_w, conv_b, dt_bias, a_log, d_ch, norm_w, w_out, *,
                batch, lt, lc, gps):
    t = proj.shape[0]
    nt = t // batch // lt
    xw = gps * GROUP_CH
    sw = gps * D_STATE

    def rows(b, i, s):
        return b * nt + i

    x_blk = D_INNER // xw
    b_blk = 2 * D_INNER // sw
    c_blk = b_blk + SSM_GROUPS * D_STATE // sw
    cw_b_blk = D_INNER // sw
    cw_c_blk = cw_b_blk + SSM_GROUPS * D_STATE // sw
    in_specs = [
        pl.BlockSpec((lt, xw), lambda b, i, s: (rows(b, i, s), s)),
        pl.BlockSpec((lt, xw), lambda b, i, s: (rows(b, i, s), x_blk + s)),
        pl.BlockSpec((lt, sw), lambda b, i, s: (rows(b, i, s), b_blk + s)),
        pl.BlockSpec((lt, sw), lambda b, i, s: (rows(b, i, s), c_blk + s)),
        pl.BlockSpec((lt, LANES), lambda b, i, s: (rows(b, i, s), 0)),
        pl.BlockSpec((lt, D_MODEL), lambda b, i, s: (rows(b, i, s), 0)),
        pl.BlockSpec((CONV_WIDTH, xw), lambda b, i, s: (0, s)),
        pl.BlockSpec((CONV_WIDTH, sw), lambda b, i, s: (0, cw_b_blk + s)),
        pl.BlockSpec((CONV_WIDTH, sw), lambda b, i, s: (0, cw_c_blk + s)),
        pl.BlockSpec((1, xw), lambda b, i, s: (0, s)),
        pl.BlockSpec((1, sw), lambda b, i, s: (0, cw_b_blk + s)),
        pl.BlockSpec((1, sw), lambda b, i, s: (0, cw_c_blk + s)),
        pl.BlockSpec((1, LANES), lambda b, i, s: (0, 0)),
        pl.BlockSpec((1, LANES), lambda b, i, s: (0, 0)),
        pl.BlockSpec((1, xw), lambda b, i, s: (0, s)),
        pl.BlockSpec((1, xw), lambda b, i, s: (0, s)),
        pl.BlockSpec((xw, D_MODEL), lambda b, i, s: (s, 0)),
    ]
    return pl.pallas_call(
        functools.partial(_mamba_core_kernel, lt=lt, lc=lc, gps=gps),
        grid=(batch, nt, SSM_GROUPS // gps),
        in_specs=in_specs,
        out_specs=pl.BlockSpec((lt, D_MODEL), lambda b, i, s: (rows(b, i, s), 0)),
        out_shape=jax.ShapeDtypeStruct((t, D_MODEL), F32),
        scratch_shapes=[
            pltpu.VMEM((SSM_GROUPS, D_STATE, GROUP_CH), F32),
            pltpu.VMEM((SSM_GROUPS, SUBLANES, GROUP_CH), F32),
            pltpu.VMEM((SSM_GROUPS, SUBLANES, D_STATE), F32),
            pltpu.VMEM((SSM_GROUPS, SUBLANES, D_STATE), F32),
            pltpu.VMEM((gps, lt + SUBLANES, GROUP_CH), F32),
            pltpu.VMEM((gps, lt + SUBLANES, D_STATE), F32),
            pltpu.VMEM((gps, lt + SUBLANES, D_STATE), F32),
            pltpu.VMEM((lt, LANES), F32),
            pltpu.VMEM((lt, LANES), F32),
            pltpu.VMEM((LANES, lt), F32),
        ],
        compiler_params=_compiler_params(("parallel", "arbitrary", "arbitrary")),
        name="mamba_core",
    )(proj, proj, proj, proj, dt_raw, x_res,
      conv_w, conv_w, conv_w, conv_b, conv_b, conv_b,
      dt_bias, a_log, d_ch, norm_w, w_out)


def _softplus2(z2):
    neg_abs = lax.bitcast_convert_type(
        lax.bitcast_convert_type(z2, jnp.uint32) | jnp.uint32(0x80000000), F32)
    return jnp.maximum(z2, 0.0) + jnp.log(1.0 + jnp.exp2(neg_abs)) * LOG2E


def _sb_attn_kernel(zmax_ref, q_ref, k_ref, v_ref, o_ref, qs_ref, carry_ref, acc_ref, zs_ref, *, blk):
    qi = pl.program_id(2)
    q = q_ref[0]
    lane = lax.broadcasted_iota(jnp.int32, (1, LANES), 1)
    first_head = lane < SB_HEAD_DIM
    zero = jnp.zeros_like(q)
    q0 = jnp.where(first_head, q, zero)
    q1 = jnp.where(first_head, zero, q)
    qs_ref[...] = jnp.concatenate([q0[:blk], q1[:blk], q0[blk:], q1[blk:]], axis=0)

    row = lax.broadcasted_iota(jnp.int32, (blk, blk), 0)
    col = lax.broadcasted_iota(jnp.int32, (blk, blk), 1)
    neg_suffix = jnp.where(row >= col, -1.0, 0.0).astype(BF16)
    row2 = lax.broadcasted_iota(jnp.int32, (2 * blk, blk), 0) & (blk - 1)
    col2 = lax.broadcasted_iota(jnp.int32, (2 * blk, blk), 1)
    diag_causal = col2 < row2

    def logits(q_rows, j):
        kj = k_ref[0, pl.ds(pl.multiple_of(j * blk, blk), blk), :]
        return lax.dot_general(q_rows, kj, (((1,), (1,)), ((), ())), preferred_element_type=F32)

    def weigh(z, j, carry, mask):
        vj = v_ref[0, pl.ds(pl.multiple_of(j * blk, blk), blk), :]
        sp = _softplus2(z)
        if mask is not None:
            sp = jnp.where(mask, sp, 0.0)
        incl = jnp.dot(sp.astype(BF16), neg_suffix, preferred_element_type=F32)
        w = jnp.exp2(z + incl + carry)
        if mask is not None:
            w = jnp.where(mask, w, 0.0)
        w = w.astype(BF16)
        lhs = jnp.concatenate(
            [jnp.concatenate([w[r:r + blk], w[r + blk:r + 2 * blk]], axis=1)
             for r in range(0, z.shape[0], 2 * blk)], axis=0)
        v_heads = jnp.concatenate([jnp.where(first_head, vj, jnp.zeros_like(vj)),
                                   jnp.where(first_head, jnp.zeros_like(vj), vj)], axis=0)
        return carry + incl[:, 0:1], jnp.dot(lhs, v_heads, preferred_element_type=F32)

    top = qs_ref[0:2 * blk, :]
    bottom = qs_ref[2 * blk:4 * blk, :]
    no_carry = jnp.zeros((2 * blk, 1), F32)
    jd = 2 * qi
    c_bot, a_bot = weigh(logits(bottom, jd + 1), jd + 1, no_carry, diag_causal)
    c_bot, a_bot2 = weigh(logits(bottom, jd), jd, c_bot, None)
    c_top, a_top = weigh(logits(top, jd), jd, no_carry, diag_causal)
    carry_ref[0:2 * blk, :] = c_top
    carry_ref[2 * blk:4 * blk, :] = c_bot
    acc_ref[0:blk, :] = a_top
    acc_ref[blk:2 * blk, :] = a_bot + a_bot2

    zs_ref[...] = logits(qs_ref[...], jnp.maximum(jd - 1, 0))

    def some_weight_may_be_nonzero():
        return (jnp.max(carry_ref[...]) + zmax_ref[0] > -UNDERFLOW_LOG2).astype(jnp.int32)

    def visit_blocks(n_iters, n_blocks, first_block):
        def step(state):
            t, _ = state
            j = first_block - n_blocks * t
            z = zs_ref[...]
            c = carry_ref[...]
            total = None
            for u in range(n_blocks):
                if u + 1 < n_blocks:
                    z_next = logits(qs_ref[...], j - u - 1)
                else:
                    zs_ref[...] = logits(qs_ref[...], jnp.maximum(j - n_blocks, 0))
                c, a = weigh(z, j - u, c, None)
                total = a if total is None else total + a
                z = z_next
            carry_ref[...] = c
            acc_ref[...] += total
            return t + 1, some_weight_may_be_nonzero()

        lax.while_loop(lambda state: (state[0] < n_iters) & (state[1] != 0), step,
                       (jnp.int32(0), some_weight_may_be_nonzero()))

    n_quads = lax.shift_right_logical(qi, 1)
    visit_blocks(n_quads, 4, jd - 1)
    visit_blocks(qi & 1, 2, jd - 1 - 4 * n_quads)
    o_ref[0] = acc_ref[...].astype(o_ref.dtype)


def _sb_attention(zmax, qkv, *, blk):
    b, l, _ = qkv.shape
    pairs = SB_HEADS * SB_HEAD_DIM // LANES
    return pl.pallas_call(
        functools.partial(_sb_attn_kernel, blk=blk),
        grid=(b, pairs, l // (2 * blk)),
        in_specs=[
            pl.BlockSpec(memory_space=pltpu.SMEM),
            pl.BlockSpec((1, 2 * blk, LANES), lambda bi, p, i: (bi, i, p)),
            pl.BlockSpec((1, l, LANES), lambda bi, p, i: (bi, 0, pairs + p)),
            pl.BlockSpec((1, l, LANES), lambda bi, p, i: (bi, 0, 2 * pairs + p)),
        ],
        out_specs=pl.BlockSpec((1, 2 * blk, LANES), lambda bi, p, i: (bi, i, p)),
        out_shape=jax.ShapeDtypeStruct((b, l, D_MODEL), BF16),
        scratch_shapes=[
            pltpu.VMEM((4 * blk, LANES), BF16),
            pltpu.VMEM((4 * blk, 1), F32),
            pltpu.VMEM((2 * blk, LANES), F32),
            pltpu.VMEM((4 * blk, blk), F32),
        ],
        compiler_params=_compiler_params(("parallel", "parallel", "arbitrary")),
        name="sb_attention",
    )(zmax, qkv, qkv, qkv)


def _proj_residual_kernel(x_ref, a_ref, w_ref, o_ref):
    o_ref[...] = x_ref[...] + jnp.dot(a_ref[...], w_ref[...], preferred_element_type=F32)


def _proj_residual(x, a, w, *, tm):
    t, d = x.shape
    k = a.shape[1]
    return pl.pallas_call(
        _proj_residual_kernel,
        grid=(t // tm,),
        in_specs=[
            pl.BlockSpec((tm, d), lambda i: (i, 0)),
            pl.BlockSpec((tm, k), lambda i: (i, 0)),
            pl.BlockSpec((k, d), lambda i: (0, 0)),
        ],
        out_specs=pl.BlockSpec((tm, d), lambda i: (i, 0)),
        out_shape=jax.ShapeDtypeStruct((t, d), F32),
        compiler_params=_compiler_params(("parallel",)),
        name="proj_residual",
    )(x, a, w)


def _ffn_kernel(x_ref, g_ref, wg_ref, wu_ref, wo_ref, o_ref, h_ref):
    j = pl.program_id(1)

    @pl.when(j == 0)
    def _():
        x = x_ref[...]
        h_ref[...] = (_rms(x, NORM_EPS) * g_ref[...]).astype(BF16)
        o_ref[...] = x

    h = h_ref[...]
    gate = jnp.dot(h, wg_ref[...], preferred_element_type=F32)
    up = jnp.dot(h, wu_ref[...], preferred_element_type=F32)
    act = (_silu(gate) * up).astype(BF16)
    o_ref[...] += jnp.dot(act, wo_ref[...], preferred_element_type=F32)


def _ffn(x, gain, w_in, w_out, *, tm, tf):
    t, d = x.shape
    nf = D_FF // tf
    weights_mode = pl.Buffered(1) if nf == 1 else None
    return pl.pallas_call(
        _ffn_kernel,
        grid=(t // tm, nf),
        in_specs=[
            pl.BlockSpec((tm, d), lambda i, j: (i, 0)),
            pl.BlockSpec((1, d), lambda i, j: (0, 0)),
            pl.BlockSpec((d, tf), lambda i, j: (0, j), pipeline_mode=weights_mode),
            pl.BlockSpec((d, tf), lambda i, j: (0, nf + j), pipeline_mode=weights_mode),
            pl.BlockSpec((tf, d), lambda i, j: (j, 0), pipeline_mode=weights_mode),
        ],
        out_specs=pl.BlockSpec((tm, d), lambda i, j: (i, 0)),
        out_shape=jax.ShapeDtypeStruct((t, d), F32),
        scratch_shapes=[pltpu.VMEM((tm, d), BF16)],
        compiler_params=_compiler_params(("parallel", "arbitrary")),
        name="swiglu_ffn",
    )(x, gain, w_in, w_in, w_out)


def _tiles(batch, seq):
    t = batch * seq
    tm = min(1024, t)
    return dict(
        proj_tm=tm,
        in_proj_tn=1024,
        qkv_tn=D_MODEL,
        mamba_lt=min(256, seq),
        mamba_lc=min(128, seq),
        mamba_gps=4,
        attn_blk=min(256, seq),
        ffn_tm=min(512, t),
        ffn_tf=D_FF,
    )


def kernel(x, norm_mix, norm_ffn, ssm_w_in, ssm_conv_w, ssm_conv_b, ssm_dt_bias, ssm_a_log, ssm_d,
           ssm_norm_w, ssm_w_out, sb_w_qkv, sb_q_gain, sb_k_gain, sb_w_o, ffn_w_in, ffn_w_out):
    batch, seq, d = x.shape
    t = batch * seq
    tl = _tiles(batch, seq)
    xf = x.reshape(t, d)

    pad_heads = LANES - SSM_HEADS
    w_in = ssm_w_in[0].astype(BF16)
    proj, dt_raw = _in_proj(xf, norm_mix[0:1], w_in[:, :IN_PROJ_COLS],
                            jnp.pad(w_in[:, IN_PROJ_COLS:], ((0, 0), (0, pad_heads))),
                            tm=tl["proj_tm"], tn=tl["in_proj_tn"])
    xf = _mamba_core(
        proj, dt_raw, xf, ssm_conv_w[0], ssm_conv_b[0:1],
        jnp.pad(ssm_dt_bias[0:1], ((0, 0), (0, pad_heads))),
        jnp.pad(ssm_a_log[0:1], ((0, 0), (0, pad_heads))),
        jnp.repeat(ssm_d[0:1], SSM_HEAD_DIM, axis=1),
        ssm_norm_w[0:1], ssm_w_out[0].astype(BF16),
        batch=batch, lt=tl["mamba_lt"], lc=tl["mamba_lc"], gps=tl["mamba_gps"])
    xf = _ffn(xf, norm_ffn[0:1], ffn_w_in[0].astype(BF16), ffn_w_out[0].astype(BF16),
              tm=tl["ffn_tm"], tf=tl["ffn_tf"])

    head_gain = jnp.concatenate([
        jnp.tile(sb_q_gain[0], SB_HEADS) * (LOG2E * SB_HEAD_DIM ** -0.5),
        jnp.tile(sb_k_gain[0], SB_HEADS),
        jnp.ones((D_MODEL,), F32)])[None, :]
    qkv = _qkv_proj(xf, norm_mix[1:2], sb_w_qkv[0].astype(BF16), head_gain,
                    tm=tl["proj_tm"], tn=tl["qkv_tn"])
    zmax = (1.1 * LOG2E * SB_HEAD_DIM ** 0.5) * jnp.max(jnp.abs(sb_q_gain[0])) * jnp.max(jnp.abs(sb_k_gain[0]))
    attn = _sb_attention(zmax.reshape(1), qkv.reshape(batch, seq, 3 * d), blk=tl["attn_blk"])
    xf = _proj_residual(xf, attn.reshape(t, d), sb_w_o[0].astype(BF16), tm=tl["proj_tm"])
    xf = _ffn(xf, norm_ffn[1:2], ffn_w_in[1].astype(BF16), ffn_w_out[1].astype(BF16),
              tm=tl["ffn_tm"], tf=tl["ffn_tf"])
    return xf.reshape(batch, seq, d)
```

```python
import functools

import jax
import jax.numpy as jnp
from jax import lax
from jax.experimental import pallas as pl
from jax.experimental.pallas import tpu as pltpu

F32 = jnp.float32
BF16 = jnp.bfloat16

D_MODEL = 1024
SSM_HEAD_DIM = 64
SSM_HEADS = 32
SSM_GROUPS = 8
SSM_HEADS_PER_GROUP = SSM_HEADS // SSM_GROUPS
D_STATE = 128
D_INNER = SSM_HEADS * SSM_HEAD_DIM
GROUP_CH = D_INNER // SSM_GROUPS
CONV_WIDTH = 4
SB_HEADS = 16
SB_HEAD_DIM = 64
D_FF = 2816
NORM_EPS = 1e-6
GATED_NORM_EPS = 1e-5
LOG2E = 1.4426950408889634
UNDERFLOW_LOG2 = 1100.0

LANES = 128
SUBLANES = 8
MXU_DIM = 256
VMEM_LIMIT_BYTES = 56 * 1024 * 1024

IN_PROJ_COLS = D_INNER + D_INNER + 2 * SSM_GROUPS * D_STATE


def _rms(x, eps):
    return x * lax.rsqrt(jnp.mean(x * x, axis=-1, keepdims=True) + eps)


def _softplus(x):
    return jnp.maximum(x, 0.0) + jnp.log(1.0 + jnp.exp(-jnp.abs(x)))


def _silu(x):
    return x / (1.0 + jnp.exp(-x))


def _compiler_params(semantics):
    return pltpu.CompilerParams(dimension_semantics=semantics, vmem_limit_bytes=VMEM_LIMIT_BYTES)


def _in_proj_kernel(x_ref, g_ref, w_ref, wdt_ref, o_ref, odt_ref, h_ref):
    @pl.when(pl.program_id(1) == 0)
    def _():
        h = (_rms(x_ref[...], NORM_EPS) * g_ref[...]).astype(BF16)
        h_ref[...] = h
        odt_ref[...] = jnp.dot(h, wdt_ref[...], preferred_element_type=F32)

    o_ref[...] = jnp.dot(h_ref[...], w_ref[...], preferred_element_type=F32)


def _in_proj(x, gain, w, w_dt, *, tm, tn):
    t, d = x.shape
    n = w.shape[1]
    return pl.pallas_call(
        _in_proj_kernel,
        grid=(t // tm, n // tn),
        in_specs=[
            pl.BlockSpec((tm, d), lambda i, j: (i, 0)),
            pl.BlockSpec((1, d), lambda i, j: (0, 0)),
            pl.BlockSpec((d, tn), lambda i, j: (0, j)),
            pl.BlockSpec((d, LANES), lambda i, j: (0, 0)),
        ],
        out_specs=[
            pl.BlockSpec((tm, tn), lambda i, j: (i, j)),
            pl.BlockSpec((tm, LANES), lambda i, j: (i, 0)),
        ],
        out_shape=[jax.ShapeDtypeStruct((t, n), F32), jax.ShapeDtypeStruct((t, LANES), F32)],
        scratch_shapes=[pltpu.VMEM((tm, d), BF16)],
        compiler_params=_compiler_params(("parallel", "arbitrary")),
        name="in_proj",
    )(x, gain, w, w_dt)


def _qkv_proj_kernel(x_ref, g_ref, w_ref, hg_ref, o_ref, h_ref, *, n_norm_tiles):
    j = pl.program_id(1)

    @pl.when(j == 0)
    def _():
        h_ref[...] = (_rms(x_ref[...], NORM_EPS) * g_ref[...]).astype(BF16)

    y = jnp.dot(h_ref[...], w_ref[...], preferred_element_type=F32)

    @pl.when(j < n_norm_tiles)
    def _():
        row = lax.broadcasted_iota(jnp.int32, (MXU_DIM, MXU_DIM), 0) // SB_HEAD_DIM
        col = lax.broadcasted_iota(jnp.int32, (MXU_DIM, MXU_DIM), 1) // SB_HEAD_DIM
        head_mean = jnp.where(row == col, 1.0 / SB_HEAD_DIM, 0.0).astype(BF16)
        y2 = (y * y).astype(BF16)
        ms = jnp.concatenate(
            [jnp.dot(y2[:, c:c + MXU_DIM], head_mean, preferred_element_type=F32)
             for c in range(0, y.shape[1], MXU_DIM)], axis=1)
        o_ref[...] = (y * lax.rsqrt(ms + NORM_EPS) * hg_ref[...]).astype(o_ref.dtype)

    @pl.when(j >= n_norm_tiles)
    def _():
        o_ref[...] = y.astype(o_ref.dtype)


def _qkv_proj(x, gain, w, head_gain, *, tm, tn):
    t, d = x.shape
    n = w.shape[1]
    return pl.pallas_call(
        functools.partial(_qkv_proj_kernel, n_norm_tiles=(2 * D_MODEL) // tn),
        grid=(t // tm, n // tn),
        in_specs=[
            pl.BlockSpec((tm, d), lambda i, j: (i, 0)),
            pl.BlockSpec((1, d), lambda i, j: (0, 0)),
            pl.BlockSpec((d, tn), lambda i, j: (0, j)),
            pl.BlockSpec((1, tn), lambda i, j: (0, j)),
        ],
        out_specs=pl.BlockSpec((tm, tn), lambda i, j: (i, j)),
        out_shape=jax.ShapeDtypeStruct((t, n), BF16),
        scratch_shapes=[pltpu.VMEM((tm, d), BF16)],
        compiler_params=_compiler_params(("parallel", "arbitrary")),
        name="qkv_proj",
    )(x, gain, w, head_gain)


def _conv_silu(u_ref, tail_ref, buf_ref, w_ref, b_ref, g, k_local, width, lt):
    lanes = slice(k_local * width, (k_local + 1) * width)
    buf = buf_ref.at[k_local]
    buf[0:SUBLANES, :] = tail_ref[g]
    buf[SUBLANES:SUBLANES + lt, :] = u_ref[:, lanes]
    tail_ref[g] = buf[lt:lt + SUBLANES, :]
    acc = b_ref[:, lanes]
    for k in range(CONV_WIDTH):
        start = SUBLANES - (CONV_WIDTH - 1) + k
        acc = acc + w_ref[k:k + 1, lanes] * buf[start:start + lt, :]
    return _silu(acc)


def _head_select(pieces):
    lane = lax.broadcasted_iota(jnp.int32, (1, LANES), 1)
    lo = lane < SSM_HEAD_DIM
    return jnp.concatenate(
        [jnp.where(lo, pieces[0], pieces[1]), jnp.where(lo, pieces[2], pieces[3])], axis=1)


def _split3_dot(tri_bf16, a):
    a0 = a.astype(BF16)
    r1 = a - a0.astype(F32)
    a1 = r1.astype(BF16)
    a2 = (r1 - a1.astype(F32)).astype(BF16)
    out = jnp.dot(tri_bf16, a0, preferred_element_type=F32)
    out = out + jnp.dot(tri_bf16, a1, preferred_element_type=F32)
    return out + jnp.dot(tri_bf16, a2, preferred_element_type=F32)


def _mamba_core_kernel(z_ref, ux_ref, ub_ref, uc_ref, dt_ref, xres_ref,
                       cwx_ref, cwb_ref, cwc_ref, cbx_ref, cbb_ref, cbc_ref,
                       dtb_ref, alog_ref, dch_ref, nw_ref, wout_ref,
                       o_ref,
                       state_ref, tailx_ref, tailb_ref, tailc_ref,
                       bufx_ref, bufb_ref, bufc_ref, dts_ref, acs_ref, acst_ref, *, lt, lc, gps):
    i = pl.program_id(1)
    gstep = pl.program_id(2)

    @pl.when(i == 0)
    def _():
        for k_local in range(gps):
            g = gstep * gps + k_local
            state_ref[g] = jnp.zeros(state_ref.shape[1:], F32)
            tailx_ref[g] = jnp.zeros(tailx_ref.shape[1:], F32)
            tailb_ref[g] = jnp.zeros(tailb_ref.shape[1:], F32)
            tailc_ref[g] = jnp.zeros(tailc_ref.shape[1:], F32)

    @pl.when(gstep == 0)
    def _():
        dt_all = _softplus(dt_ref[...] + dtb_ref[...])
        a = dt_all * (-jnp.exp(alog_ref[...]))
        row = lax.broadcasted_iota(jnp.int32, (lt, lt), 0)
        col = lax.broadcasted_iota(jnp.int32, (lt, lt), 1)
        same_chunk_causal = (row >= col) & ((row // lc) == (col // lc))
        acs_all = _split3_dot(same_chunk_causal.astype(BF16), a)
        dts_ref[...] = dt_all
        acs_ref[...] = acs_all
        acst_ref[...] = acs_all.T

    dt = dts_ref[...]
    acs = acs_ref[...]
    lane = lax.broadcasted_iota(jnp.int32, (1, LANES), 1)
    row_c = lax.broadcasted_iota(jnp.int32, (lc, lc), 0)
    col_c = lax.broadcasted_iota(jnp.int32, (lc, lc), 1)
    causal = row_c >= col_c
    second_head = (lane // SSM_HEAD_DIM) == 1
    normed = [_mamba_group(k_local, gstep * gps + k_local, dt, acs, lane, causal, second_head,
                           z_ref, ux_ref, ub_ref, uc_ref, cwx_ref, cwb_ref, cwc_ref,
                           cbx_ref, cbb_ref, cbc_ref, dch_ref, nw_ref,
                           state_ref, tailx_ref, tailb_ref, tailc_ref,
                           bufx_ref, bufb_ref, bufc_ref, acst_ref, lt, lc)
              for k_local in range(gps)]
    contrib = jnp.dot(jnp.concatenate(normed, axis=1), wout_ref[...], preferred_element_type=F32)

    @pl.when(gstep == 0)
    def _():
        o_ref[...] = xres_ref[...] + contrib

    @pl.when(gstep != 0)
    def _():
        o_ref[...] += contrib


def _mamba_group(k_local, g, dt, acs, lane, causal, second_head,
                 z_ref, ux_ref, ub_ref, uc_ref, cwx_ref, cwb_ref, cwc_ref,
                 cbx_ref, cbb_ref, cbc_ref, dch_ref, nw_ref,
                 state_ref, tailx_ref, tailb_ref, tailc_ref,
                 bufx_ref, bufb_ref, bufc_ref, acst_ref, lt, lc):
    ch = slice(k_local * GROUP_CH, (k_local + 1) * GROUP_CH)
    xs = _conv_silu(ux_ref, tailx_ref, bufx_ref, cwx_ref, cbx_ref, g, k_local, GROUP_CH, lt)
    bm = _conv_silu(ub_ref, tailb_ref, bufb_ref, cwb_ref, cbb_ref, g, k_local, D_STATE, lt)
    cm = _conv_silu(uc_ref, tailc_ref, bufc_ref, cwc_ref, cbc_ref, g, k_local, D_STATE, lt)
    bm16 = bm.astype(BF16)
    cm16 = cm.astype(BF16)

    dt_cols, acs_cols, acs_rows = [], [], []
    for r in range(SSM_HEADS_PER_GROUP):
        h = g * SSM_HEADS_PER_GROUP + r
        sel = lane == h
        dt_cols.append(jnp.sum(jnp.where(sel, dt, 0.0), axis=1, keepdims=True))
        acs_cols.append(jnp.sum(jnp.where(sel, acs, 0.0), axis=1, keepdims=True))
        acs_rows.append(acst_ref[pl.ds(h, 1), :])

    dt_x = _head_select(dt_cols)
    acs_x = _head_select(acs_cols)
    xdt = xs * dt_x
    decay_in = jnp.exp(acs_x)

    state = state_ref[g]
    y_chunks = []
    for c in range(lt // lc):
        lo, hi = c * lc, (c + 1) * lc
        cb = lax.dot_general(cm16[lo:hi], bm16[lo:hi], (((1,), (1,)), ((), ())),
                             preferred_element_type=F32)
        y_tiles = [None, None]
        for r in range(SSM_HEADS_PER_GROUP):
            seg = acs_cols[r][lo:hi] - acs_rows[r][:, lo:hi]
            lmat = jnp.exp(jnp.where(causal, seg, -jnp.inf))
            gm = (cb * lmat).astype(BF16)
            tile = r // 2
            x_tile = xdt[lo:hi, tile * LANES:(tile + 1) * LANES]
            keep = second_head if r % 2 else jnp.logical_not(second_head)
            part = jnp.dot(gm, jnp.where(keep, x_tile, 0.0).astype(BF16), preferred_element_type=F32)
            y_tiles[tile] = part if y_tiles[tile] is None else y_tiles[tile] + part
        y_inter = jnp.dot(cm16[lo:hi], state.astype(BF16), preferred_element_type=F32)
        y_chunks.append(jnp.concatenate(y_tiles, axis=1) + y_inter * decay_in[lo:hi])
        last_x = acs_x[hi - 1:hi, :]
        w_end = (xdt[lo:hi] * jnp.exp(last_x - acs_x[lo:hi])).astype(BF16)
        upd = lax.dot_general(bm16[lo:hi], w_end, (((0,), (0,)), ((), ())),
                              preferred_element_type=F32)
        state = state * jnp.exp(last_x) + upd
    state_ref[g] = state
    y = jnp.concatenate(y_chunks, axis=0) + dch_ref[:, ch] * xs

    yg = y * _silu(z_ref[:, ch])
    return (_rms(yg, GATED_NORM_EPS) * nw_ref[:, ch]).astype(BF16)


def _mamba_core(proj, dt_raw, x_res, conv_w, conv_b, dt_bias, a_log, d_ch, norm_w, w_out, *,
                batch, lt, lc, gps):
    t = proj.shape[0]
    nt = t // batch // lt
    xw = gps * GROUP_CH
    sw = gps * D_STATE

    def rows(b, i, s):
        return b * nt + i

    x_blk = D_INNER // xw
    b_blk = 2 * D_INNER // sw
    c_blk = b_blk + SSM_GROUPS * D_STATE // sw
    cw_b_blk = D_INNER // sw
    cw_c_blk = cw_b_blk + SSM_GROUPS * D_STATE // sw
    in_specs = [
        pl.BlockSpec((lt, xw), lambda b, i, s: (rows(b, i, s), s)),
        pl.BlockSpec((lt, xw), lambda b, i, s: (rows(b, i, s), x_blk + s)),
        pl.BlockSpec((lt, sw), lambda b, i, s: (rows(b, i, s), b_blk + s)),
        pl.BlockSpec((lt, sw), lambda b, i, s: (rows(b, i, s), c_blk + s)),
        pl.BlockSpec((lt, LANES), lambda b, i, s: (rows(b, i, s), 0)),
        pl.BlockSpec((lt, D_MODEL), lambda b, i, s: (rows(b, i, s), 0)),
        pl.BlockSpec((CONV_WIDTH, xw), lambda b, i, s: (0, s)),
        pl.BlockSpec((CONV_WIDTH, sw), lambda b, i, s: (0, cw_b_blk + s)),
        pl.BlockSpec((CONV_WIDTH, sw), lambda b, i, s: (0, cw_c_blk + s)),
        pl.BlockSpec((1, xw), lambda b, i, s: (0, s)),
        pl.BlockSpec((1, sw), lambda b, i, s: (0, cw_b_blk + s)),
        pl.BlockSpec((1, sw), lambda b, i, s: (0, cw_c_blk + s)),
        pl.BlockSpec((1, LANES), lambda b, i, s: (0, 0)),
        pl.BlockSpec((1, LANES), lambda b, i, s: (0, 0)),
        pl.BlockSpec((1, xw), lambda b, i, s: (0, s)),
        pl.BlockSpec((1, xw), lambda b, i, s: (0, s)),
        pl.BlockSpec((xw, D_MODEL), lambda b, i, s: (s, 0)),
    ]
    return pl.pallas_call(
        functools.partial(_mamba_core_kernel, lt=lt, lc=lc, gps=gps),
        grid=(batch, nt, SSM_GROUPS // gps),
        in_specs=in_specs,
        out_specs=pl.BlockSpec((lt, D_MODEL), lambda b, i, s: (rows(b, i, s), 0)),
        out_shape=jax.ShapeDtypeStruct((t, D_MODEL), F32),
        scratch_shapes=[
            pltpu.VMEM((SSM_GROUPS, D_STATE, GROUP_CH), F32),
            pltpu.VMEM((SSM_GROUPS, SUBLANES, GROUP_CH), F32),
            pltpu.VMEM((SSM_GROUPS, SUBLANES, D_STATE), F32),
            pltpu.VMEM((SSM_GROUPS, SUBLANES, D_STATE), F32),
            pltpu.VMEM((gps, lt + SUBLANES, GROUP_CH), F32),
            pltpu.VMEM((gps, lt + SUBLANES, D_STATE), F32),
            pltpu.VMEM((gps, lt + SUBLANES, D_STATE), F32),
            pltpu.VMEM((lt, LANES), F32),
            pltpu.VMEM((lt, LANES), F32),
            pltpu.VMEM((LANES, lt), F32),
        ],
        compiler_params=_compiler_params(("parallel", "arbitrary", "arbitrary")),
        name="mamba_core",
    )(proj, proj, proj, proj, dt_raw, x_res,
      conv_w, conv_w, conv_w, conv_b, conv_b, conv_b,
      dt_bias, a_log, d_ch, norm_w, w_out)


def _softplus2(z2):
    neg_abs = lax.bitcast_convert_type(
        lax.bitcast_convert_type(z2, jnp.uint32) | jnp.uint32(0x80000000), F32)
    return jnp.maximum(z2, 0.0) + jnp.log(1.0 + jnp.exp2(neg_abs)) * LOG2E


def _sb_attn_kernel(zmax_ref, q_ref, k_ref, v_ref, o_ref, qs_ref, carry_ref, acc_ref, zs_ref, *,
                    blk, n_sub):
    lane = lax.broadcasted_iota(jnp.int32, (1, LANES), 1)
    first_head = lane < SB_HEAD_DIM
    row = lax.broadcasted_iota(jnp.int32, (blk, blk), 0)
    col = lax.broadcasted_iota(jnp.int32, (blk, blk), 1)
    neg_suffix = jnp.where(row >= col, -1.0, 0.0).astype(BF16)
    row2 = lax.broadcasted_iota(jnp.int32, (2 * blk, blk), 0) & (blk - 1)
    col2 = lax.broadcasted_iota(jnp.int32, (2 * blk, blk), 1)
    diag_causal = col2 < row2

    def logits(q_rows, j):
        kj = k_ref[0, pl.ds(pl.multiple_of(j * blk, blk), blk), :]
        return lax.dot_general(q_rows, kj, (((1,), (1,)), ((), ())), preferred_element_type=F32)

    def weigh(z, j, carry, mask):
        vj = v_ref[0, pl.ds(pl.multiple_of(j * blk, blk), blk), :]
        sp = _softplus2(z)
        if mask is not None:
            sp = jnp.where(mask, sp, 0.0)
        incl = jnp.dot(sp.astype(BF16), neg_suffix, preferred_element_type=F32)
        w = jnp.exp2(z + incl + carry)
        if mask is not None:
            w = jnp.where(mask, w, 0.0)
        w = w.astype(BF16)
        lhs = jnp.concatenate(
            [jnp.concatenate([w[r:r + blk], w[r + blk:r + 2 * blk]], axis=1)
             for r in range(0, z.shape[0], 2 * blk)], axis=0)
        v_heads = jnp.concatenate([jnp.where(first_head, vj, jnp.zeros_like(vj)),
                                   jnp.where(first_head, jnp.zeros_like(vj), vj)], axis=0)
        return carry + incl[:, 0:1], jnp.dot(lhs, v_heads, preferred_element_type=F32)

    def diagonal(sub):
        qi = pl.program_id(2) * n_sub + sub
        qs, carry_s, acc_s, zs = qs_ref.at[sub], carry_ref.at[sub], acc_ref.at[sub], zs_ref.at[sub]
        q = q_ref[0, sub * 2 * blk:(sub + 1) * 2 * blk, :]
        zero = jnp.zeros_like(q)
        q0 = jnp.where(first_head, q, zero)
        q1 = jnp.where(first_head, zero, q)
        qs[...] = jnp.concatenate([q0[:blk], q1[:blk], q0[blk:], q1[blk:]], axis=0)
        top = qs[0:2 * blk, :]
        bottom = qs[2 * blk:4 * blk, :]
        no_carry = jnp.zeros((2 * blk, 1), F32)
        jd = 2 * qi
        c_bot, a_bot = weigh(logits(bottom, jd + 1), jd + 1, no_carry, diag_causal)
        c_bot, a_bot2 = weigh(logits(bottom, jd), jd, c_bot, None)
        c_top, a_top = weigh(logits(top, jd), jd, no_carry, diag_causal)
        carry_s[0:2 * blk, :] = c_top
        carry_s[2 * blk:4 * blk, :] = c_bot
        acc_s[0:blk, :] = a_top
        acc_s[blk:2 * blk, :] = a_bot + a_bot2
        zs[...] = logits(qs[...], jnp.maximum(jd - 1, 0))

    def walk(sub):
        qi = pl.program_id(2) * n_sub + sub
        qs, carry_s, acc_s, zs = qs_ref.at[sub], carry_ref.at[sub], acc_ref.at[sub], zs_ref.at[sub]
        jd = 2 * qi

        def some_weight_may_be_nonzero():
            return (jnp.max(carry_s[...]) + zmax_ref[0] > -UNDERFLOW_LOG2).astype(jnp.int32)

        def visit_blocks(n_iters, n_blocks, first_block):
            def step(state):
                t, _ = state
                j = first_block - n_blocks * t
                z = zs[...]
                c = carry_s[...]
                total = None
                for u in range(n_blocks):
                    if u + 1 < n_blocks:
                        z_next = logits(qs[...], j - u - 1)
                    else:
                        zs[...] = logits(qs[...], jnp.maximum(j - n_blocks, 0))
                    c, a = weigh(z, j - u, c, None)
                    total = a if total is None else total + a
                    z = z_next
                carry_s[...] = c
                acc_s[...] += total
                return t + 1, some_weight_may_be_nonzero()

            lax.while_loop(lambda state: (state[0] < n_iters) & (state[1] != 0), step,
                           (jnp.int32(0), some_weight_may_be_nonzero()))

        n_quads = jnp.minimum(lax.shift_right_logical(qi, 1), 1)
        visit_blocks(n_quads, 4, jd - 1)
        visit_blocks(qi - 2 * n_quads, 2, jd - 1 - 4 * n_quads)
        o_ref[0, sub * 2 * blk:(sub + 1) * 2 * blk, :] = acc_s[...].astype(o_ref.dtype)

    for sub in range(n_sub):
        diagonal(sub)
    for sub in range(n_sub):
        walk(sub)


def _sb_attention(zmax, qkv, *, blk, n_sub):
    b, l, _ = qkv.shape
    pairs = SB_HEADS * SB_HEAD_DIM // LANES
    rows = 2 * blk * n_sub
    return pl.pallas_call(
        functools.partial(_sb_attn_kernel, blk=blk, n_sub=n_sub),
        grid=(b, pairs, l // rows),
        in_specs=[
            pl.BlockSpec(memory_space=pltpu.SMEM),
            pl.BlockSpec((1, rows, LANES), lambda bi, p, i: (bi, i, p)),
            pl.BlockSpec((1, l, LANES), lambda bi, p, i: (bi, 0, pairs + p)),
            pl.BlockSpec((1, l, LANES), lambda bi, p, i: (bi, 0, 2 * pairs + p)),
        ],
        out_specs=pl.BlockSpec((1, rows, LANES), lambda bi, p, i: (bi, i, p)),
        out_shape=jax.ShapeDtypeStruct((b, l, D_MODEL), BF16),
        scratch_shapes=[
            pltpu.VMEM((n_sub, 4 * blk, LANES), BF16),
            pltpu.VMEM((n_sub, 4 * blk, 1), F32),
            pltpu.VMEM((n_sub, 2 * blk, LANES), F32),
            pltpu.VMEM((n_sub, 4 * blk, blk), F32),
        ],
        compiler_params=_compiler_params(("parallel", "parallel", "arbitrary")),
        name="sb_attention",
    )(zmax, qkv, qkv, qkv)


def _proj_residual_kernel(x_ref, a_ref, w_ref, o_ref):
    o_ref[...] = x_ref[...] + jnp.dot(a_ref[...], w_ref[...], preferred_element_type=F32)


def _proj_residual(x, a, w, *, tm):
    t, d = x.shape
    k = a.shape[1]
    return pl.pallas_call(
        _proj_residual_kernel,
        grid=(t // tm,),
        in_specs=[
            pl.BlockSpec((tm, d), lambda i: (i, 0)),
            pl.BlockSpec((tm, k), lambda i: (i, 0)),
            pl.BlockSpec((k, d), lambda i: (0, 0)),
        ],
        out_specs=pl.BlockSpec((tm, d), lambda i: (i, 0)),
        out_shape=jax.ShapeDtypeStruct((t, d), F32),
        compiler_params=_compiler_params(("parallel",)),
        name="proj_residual",
    )(x, a, w)


def _ffn_kernel(x_ref, g_ref, wg_ref, wu_ref, wo_ref, o_ref, h_ref):
    j = pl.program_id(1)

    @pl.when(j == 0)
    def _():
        x = x_ref[...]
        h_ref[...] = (_rms(x, NORM_EPS) * g_ref[...]).astype(BF16)
        o_ref[...] = x

    h = h_ref[...]
    gate = jnp.dot(h, wg_ref[...], preferred_element_type=F32)
    up = jnp.dot(h, wu_ref[...], preferred_element_type=F32)
    act = (_silu(gate) * up).astype(BF16)
    o_ref[...] += jnp.dot(act, wo_ref[...], preferred_element_type=F32)


def _ffn(x, gain, w_in, w_out, *, tm, tf):
    t, d = x.shape
    nf = D_FF // tf
    weights_mode = pl.Buffered(1) if nf == 1 else None
    return pl.pallas_call(
        _ffn_kernel,
        grid=(t // tm, nf),
        in_specs=[
            pl.BlockSpec((tm, d), lambda i, j: (i, 0)),
            pl.BlockSpec((1, d), lambda i, j: (0, 0)),
            pl.BlockSpec((d, tf), lambda i, j: (0, j), pipeline_mode=weights_mode),
            pl.BlockSpec((d, tf), lambda i, j: (0, nf + j), pipeline_mode=weights_mode),
            pl.BlockSpec((tf, d), lambda i, j: (j, 0), pipeline_mode=weights_mode),
        ],
        out_specs=pl.BlockSpec((tm, d), lambda i, j: (i, 0)),
        out_shape=jax.ShapeDtypeStruct((t, d), F32),
        scratch_shapes=[pltpu.VMEM((tm, d), BF16)],
        compiler_params=_compiler_params(("parallel", "arbitrary")),
        name="swiglu_ffn",
    )(x, gain, w_in, w_in, w_out)


def _tiles(batch, seq):
    t = batch * seq
    tm = min(1024, t)
    return dict(
        proj_tm=tm,
        in_proj_tm=min(2048, t),
        in_proj_tn=1024,
        qkv_tn=D_MODEL,
        mamba_lt=min(256, seq),
        mamba_lc=min(128, seq),
        mamba_gps=4,
        attn_blk=min(256, seq),
        attn_sub=2 if seq % (4 * min(256, seq)) == 0 else 1,
        ffn_tm=min(512, t),
        ffn_tf=D_FF,
    )


def kernel(x, norm_mix, norm_ffn, ssm_w_in, ssm_conv_w, ssm_conv_b, ssm_dt_bias, ssm_a_log, ssm_d,
           ssm_norm_w, ssm_w_out, sb_w_qkv, sb_q_gain, sb_k_gain, sb_w_o, ffn_w_in, ffn_w_out):
    batch, seq, d = x.shape
    t = batch * seq
    tl = _tiles(batch, seq)
    xf = x.reshape(t, d)

    pad_heads = LANES - SSM_HEADS
    w_in = ssm_w_in[0].astype(BF16)
    proj, dt_raw = _in_proj(xf, norm_mix[0:1], w_in[:, :IN_PROJ_COLS],
                            jnp.pad(w_in[:, IN_PROJ_COLS:], ((0, 0), (0, pad_heads))),
                            tm=tl["in_proj_tm"], tn=tl["in_proj_tn"])
    xf = _mamba_core(
        proj, dt_raw, xf, ssm_conv_w[0], ssm_conv_b[0:1],
        jnp.pad(ssm_dt_bias[0:1], ((0, 0), (0, pad_heads))),
        jnp.pad(ssm_a_log[0:1], ((0, 0), (0, pad_heads))),
        jnp.repeat(ssm_d[0:1], SSM_HEAD_DIM, axis=1),
        ssm_norm_w[0:1], ssm_w_out[0].astype(BF16),
        batch=batch, lt=tl["mamba_lt"], lc=tl["mamba_lc"], gps=tl["mamba_gps"])
    xf = _ffn(xf, norm_ffn[0:1], ffn_w_in[0].astype(BF16), ffn_w_out[0].astype(BF16),
              tm=tl["ffn_tm"], tf=tl["ffn_tf"])

    head_gain = jnp.concatenate([
        jnp.tile(sb_q_gain[0], SB_HEADS) * (LOG2E * SB_HEAD_DIM ** -0.5),
        jnp.tile(sb_k_gain[0], SB_HEADS),
        jnp.ones((D_MODEL,), F32)])[None, :]
    qkv = _qkv_proj(xf, norm_mix[1:2], sb_w_qkv[0].astype(BF16), head_gain,
                    tm=tl["proj_tm"], tn=tl["qkv_tn"])
    zmax = (1.1 * LOG2E * SB_HEAD_DIM ** 0.5) * jnp.max(jnp.abs(sb_q_gain[0])) * jnp.max(jnp.abs(sb_k_gain[0]))
    attn = _sb_attention(zmax.reshape(1), qkv.reshape(batch, seq, 3 * d), blk=tl["attn_blk"],
                         n_sub=tl["attn_sub"])
    xf = _proj_residual(xf, attn.reshape(t, d), sb_w_o[0].astype(BF16), tm=tl["proj_tm"])
    xf = _ffn(xf, norm_ffn[1:2], ffn_w_in[1].astype(BF16), ffn_w_out[1].astype(BF16),
              tm=tl["ffn_tm"], tf=tl["ffn_tf"])
    return xf.reshape(batch, seq, d)
```

```python
import functools

import jax
import jax.numpy as jnp
from jax import lax
from jax.experimental import pallas as pl
from jax.experimental.pallas import tpu as pltpu

F32 = jnp.float32
BF16 = jnp.bfloat16

D_MODEL = 1024
SSM_HEAD_DIM = 64
SSM_HEADS = 32
SSM_GROUPS = 8
SSM_HEADS_PER_GROUP = SSM_HEADS // SSM_GROUPS
D_STATE = 128
D_INNER = SSM_HEADS * SSM_HEAD_DIM
GROUP_CH = D_INNER // SSM_GROUPS
CONV_WIDTH = 4
SB_HEADS = 16
SB_HEAD_DIM = 64
D_FF = 2816
NORM_EPS = 1e-6
GATED_NORM_EPS = 1e-5
LOG2E = 1.4426950408889634
UNDERFLOW_LOG2 = 1100.0

LANES = 128
SUBLANES = 8
MXU_DIM = 256
VMEM_LIMIT_BYTES = 56 * 1024 * 1024

IN_PROJ_COLS = D_INNER + D_INNER + 2 * SSM_GROUPS * D_STATE


def _rms(x, eps):
    return x * lax.rsqrt(jnp.mean(x * x, axis=-1, keepdims=True) + eps)


def _softplus(x):
    return jnp.maximum(x, 0.0) + jnp.log(1.0 + jnp.exp(-jnp.abs(x)))


def _silu(x):
    return x / (1.0 + jnp.exp(-x))


def _compiler_params(semantics):
    return pltpu.CompilerParams(dimension_semantics=semantics, vmem_limit_bytes=VMEM_LIMIT_BYTES)


def _in_proj_kernel(x_ref, g_ref, w_ref, wdt_ref, o_ref, odt_ref, h_ref):
    @pl.when(pl.program_id(1) == 0)
    def _():
        h = (_rms(x_ref[...], NORM_EPS) * g_ref[...]).astype(BF16)
        h_ref[...] = h
        odt_ref[...] = jnp.dot(h, wdt_ref[...], preferred_element_type=F32)

    o_ref[...] = jnp.dot(h_ref[...], w_ref[...], preferred_element_type=F32)


def _in_proj(x, gain, w, w_dt, *, tm, tn):
    t, d = x.shape
    n = w.shape[1]
    return pl.pallas_call(
        _in_proj_kernel,
        grid=(t // tm, n // tn),
        in_specs=[
            pl.BlockSpec((tm, d), lambda i, j: (i, 0)),
            pl.BlockSpec((1, d), lambda i, j: (0, 0)),
            pl.BlockSpec((d, tn), lambda i, j: (0, j)),
            pl.BlockSpec((d, LANES), lambda i, j: (0, 0)),
        ],
        out_specs=[
            pl.BlockSpec((tm, tn), lambda i, j: (i, j)),
            pl.BlockSpec((tm, LANES), lambda i, j: (i, 0)),
        ],
        out_shape=[jax.ShapeDtypeStruct((t, n), F32), jax.ShapeDtypeStruct((t, LANES), F32)],
        scratch_shapes=[pltpu.VMEM((tm, d), BF16)],
        compiler_params=_compiler_params(("parallel", "arbitrary")),
        name="in_proj",
    )(x, gain, w, w_dt)


def _qkv_proj_kernel(x_ref, g_ref, w_ref, hg_ref, o_ref, h_ref, *, n_norm_tiles):
    j = pl.program_id(1)

    @pl.when(j == 0)
    def _():
        h_ref[...] = (_rms(x_ref[...], NORM_EPS) * g_ref[...]).astype(BF16)

    y = jnp.dot(h_ref[...], w_ref[...], preferred_element_type=F32)

    @pl.when(j < n_norm_tiles)
    def _():
        row = lax.broadcasted_iota(jnp.int32, (MXU_DIM, MXU_DIM), 0) // SB_HEAD_DIM
        col = lax.broadcasted_iota(jnp.int32, (MXU_DIM, MXU_DIM), 1) // SB_HEAD_DIM
        head_mean = jnp.where(row == col, 1.0 / SB_HEAD_DIM, 0.0).astype(BF16)
        y2 = (y * y).astype(BF16)
        ms = jnp.concatenate(
            [jnp.dot(y2[:, c:c + MXU_DIM], head_mean, preferred_element_type=F32)
             for c in range(0, y.shape[1], MXU_DIM)], axis=1)
        o_ref[...] = (y * lax.rsqrt(ms + NORM_EPS) * hg_ref[...]).astype(o_ref.dtype)

    @pl.when(j >= n_norm_tiles)
    def _():
        o_ref[...] = y.astype(o_ref.dtype)


def _qkv_proj(x, gain, w, head_gain, *, tm, tn):
    t, d = x.shape
    n = w.shape[1]
    return pl.pallas_call(
        functools.partial(_qkv_proj_kernel, n_norm_tiles=(2 * D_MODEL) // tn),
        grid=(t // tm, n // tn),
        in_specs=[
            pl.BlockSpec((tm, d), lambda i, j: (i, 0)),
            pl.BlockSpec((1, d), lambda i, j: (0, 0)),
            pl.BlockSpec((d, tn), lambda i, j: (0, j)),
            pl.BlockSpec((1, tn), lambda i, j: (0, j)),
        ],
        out_specs=pl.BlockSpec((tm, tn), lambda i, j: (i, j)),
        out_shape=jax.ShapeDtypeStruct((t, n), BF16),
        scratch_shapes=[pltpu.VMEM((tm, d), BF16)],
        compiler_params=_compiler_params(("parallel", "arbitrary")),
        name="qkv_proj",
    )(x, gain, w, head_gain)


def _conv_silu(u_ref, tail_ref, buf_ref, w_ref, b_ref, g, k_local, width, lt):
    lanes = slice(k_local * width, (k_local + 1) * width)
    buf = buf_ref.at[k_local]
    buf[0:SUBLANES, :] = tail_ref[g]
    buf[SUBLANES:SUBLANES + lt, :] = u_ref[:, lanes]
    tail_ref[g] = buf[lt:lt + SUBLANES, :]
    acc = b_ref[:, lanes]
    for k in range(CONV_WIDTH):
        start = SUBLANES - (CONV_WIDTH - 1) + k
        acc = acc + w_ref[k:k + 1, lanes] * buf[start:start + lt, :]
    return _silu(acc)


def _head_select(pieces):
    lane = lax.broadcasted_iota(jnp.int32, (1, LANES), 1)
    lo = lane < SSM_HEAD_DIM
    return jnp.concatenate(
        [jnp.where(lo, pieces[0], pieces[1]), jnp.where(lo, pieces[2], pieces[3])], axis=1)


def _split3_dot(tri_bf16, a):
    a0 = a.astype(BF16)
    r1 = a - a0.astype(F32)
    a1 = r1.astype(BF16)
    a2 = (r1 - a1.astype(F32)).astype(BF16)
    out = jnp.dot(tri_bf16, a0, preferred_element_type=F32)
    out = out + jnp.dot(tri_bf16, a1, preferred_element_type=F32)
    return out + jnp.dot(tri_bf16, a2, preferred_element_type=F32)


def _mamba_core_kernel(z_ref, ux_ref, ub_ref, uc_ref, dt_ref, xres_ref,
                       cwx_ref, cwb_ref, cwc_ref, cbx_ref, cbb_ref, cbc_ref,
                       dtb_ref, alog_ref, dch_ref, nw_ref, wout_ref,
                       o_ref,
                       state_ref, tailx_ref, tailb_ref, tailc_ref,
                       bufx_ref, bufb_ref, bufc_ref, dts_ref, acs_ref, acst_ref, *, lt, lc, gps):
    i = pl.program_id(1)
    gstep = pl.program_id(2)

    @pl.when(i == 0)
    def _():
        for k_local in range(gps):
            g = gstep * gps + k_local
            state_ref[g] = jnp.zeros(state_ref.shape[1:], F32)
            tailx_ref[g] = jnp.zeros(tailx_ref.shape[1:], F32)
            tailb_ref[g] = jnp.zeros(tailb_ref.shape[1:], F32)
            tailc_ref[g] = jnp.zeros(tailc_ref.shape[1:], F32)

    @pl.when(gstep == 0)
    def _():
        dt_all = _softplus(dt_ref[...] + dtb_ref[...])
        a = dt_all * (-jnp.exp(alog_ref[...]))
        row = lax.broadcasted_iota(jnp.int32, (lt, lt), 0)
        col = lax.broadcasted_iota(jnp.int32, (lt, lt), 1)
        same_chunk_causal = (row >= col) & ((row // lc) == (col // lc))
        acs_all = _split3_dot(same_chunk_causal.astype(BF16), a)
        dts_ref[...] = dt_all
        acs_ref[...] = acs_all
        acst_ref[...] = acs_all.T

    dt = dts_ref[...]
    acs = acs_ref[...]
    lane = lax.broadcasted_iota(jnp.int32, (1, LANES), 1)
    row_c = lax.broadcasted_iota(jnp.int32, (lc, lc), 0)
    col_c = lax.broadcasted_iota(jnp.int32, (lc, lc), 1)
    causal = row_c >= col_c
    second_head = (lane // SSM_HEAD_DIM) == 1
    normed = [_mamba_group(k_local, gstep * gps + k_local, dt, acs, lane, causal, second_head,
                           z_ref, ux_ref, ub_ref, uc_ref, cwx_ref, cwb_ref, cwc_ref,
                           cbx_ref, cbb_ref, cbc_ref, dch_ref, nw_ref,
                           state_ref, tailx_ref, tailb_ref, tailc_ref,
                           bufx_ref, bufb_ref, bufc_ref, acst_ref, lt, lc)
              for k_local in range(gps)]
    contrib = jnp.dot(jnp.concatenate(normed, axis=1), wout_ref[...], preferred_element_type=F32)

    @pl.when(gstep == 0)
    def _():
        o_ref[...] = xres_ref[...] + contrib

    @pl.when(gstep != 0)
    def _():
        o_ref[...] += contrib


def _mamba_group(k_local, g, dt, acs, lane, causal, second_head,
                 z_ref, ux_ref, ub_ref, uc_ref, cwx_ref, cwb_ref, cwc_ref,
                 cbx_ref, cbb_ref, cbc_ref, dch_ref, nw_ref,
                 state_ref, tailx_ref, tailb_ref, tailc_ref,
                 bufx_ref, bufb_ref, bufc_ref, acst_ref, lt, lc):
    ch = slice(k_local * GROUP_CH, (k_local + 1) * GROUP_CH)
    xs = _conv_silu(ux_ref, tailx_ref, bufx_ref, cwx_ref, cbx_ref, g, k_local, GROUP_CH, lt)
    bm = _conv_silu(ub_ref, tailb_ref, bufb_ref, cwb_ref, cbb_ref, g, k_local, D_STATE, lt)
    cm = _conv_silu(uc_ref, tailc_ref, bufc_ref, cwc_ref, cbc_ref, g, k_local, D_STATE, lt)
    bm16 = bm.astype(BF16)
    cm16 = cm.astype(BF16)

    dt_cols, acs_cols, acs_rows = [], [], []
    for r in range(SSM_HEADS_PER_GROUP):
        h = g * SSM_HEADS_PER_GROUP + r
        sel = lane == h
        dt_cols.append(jnp.sum(jnp.where(sel, dt, 0.0), axis=1, keepdims=True))
        acs_cols.append(jnp.sum(jnp.where(sel, acs, 0.0), axis=1, keepdims=True))
        acs_rows.append(acst_ref[pl.ds(h, 1), :])

    dt_x = _head_select(dt_cols)
    acs_x = _head_select(acs_cols)
    xdt = xs * dt_x
    decay_in = jnp.exp(acs_x)

    state = state_ref[g]
    y_chunks = []
    for c in range(lt // lc):
        lo, hi = c * lc, (c + 1) * lc
        cb = lax.dot_general(cm16[lo:hi], bm16[lo:hi], (((1,), (1,)), ((), ())),
                             preferred_element_type=F32)
        y_tiles = [None, None]
        for r in range(SSM_HEADS_PER_GROUP):
            seg = acs_cols[r][lo:hi] - acs_rows[r][:, lo:hi]
            lmat = jnp.exp(jnp.where(causal, seg, -jnp.inf))
            gm = (cb * lmat).astype(BF16)
            tile = r // 2
            x_tile = xdt[lo:hi, tile * LANES:(tile + 1) * LANES]
            keep = second_head if r % 2 else jnp.logical_not(second_head)
            part = jnp.dot(gm, jnp.where(keep, x_tile, 0.0).astype(BF16), preferred_element_type=F32)
            y_tiles[tile] = part if y_tiles[tile] is None else y_tiles[tile] + part
        y_inter = jnp.dot(cm16[lo:hi], state.astype(BF16), preferred_element_type=F32)
        y_chunks.append(jnp.concatenate(y_tiles, axis=1) + y_inter * decay_in[lo:hi])
        last_x = acs_x[hi - 1:hi, :]
        w_end = (xdt[lo:hi] * jnp.exp(last_x - acs_x[lo:hi])).astype(BF16)
        upd = lax.dot_general(bm16[lo:hi], w_end, (((0,), (0,)), ((), ())),
                              preferred_element_type=F32)
        state = state * jnp.exp(last_x) + upd
    state_ref[g] = state
    y = jnp.concatenate(y_chunks, axis=0) + dch_ref[:, ch] * xs

    yg = y * _silu(z_ref[:, ch])
    return (_rms(yg, GATED_NORM_EPS) * nw_ref[:, ch]).astype(BF16)


def _mamba_core(proj, dt_raw, x_res, conv_w, conv_b, dt_bias, a_log, d_ch, norm_w, w_out, *,
                batch, lt, lc, gps):
    t = proj.shape[0]
    nt = t // batch // lt
    xw = gps * GROUP_CH
    sw = gps * D_STATE

    def rows(b, i, s):
        return b * nt + i

    x_blk = D_INNER // xw
    b_blk = 2 * D_INNER // sw
    c_blk = b_blk + SSM_GROUPS * D_STATE // sw
    cw_b_blk = D_INNER // sw
    cw_c_blk = cw_b_blk + SSM_GROUPS * D_STATE // sw
    in_specs = [
        pl.BlockSpec((lt, xw), lambda b, i, s: (rows(b, i, s), s)),
        pl.BlockSpec((lt, xw), lambda b, i, s: (rows(b, i, s), x_blk + s)),
        pl.BlockSpec((lt, sw), lambda b, i, s: (rows(b, i, s), b_blk + s)),
        pl.BlockSpec((lt, sw), lambda b, i, s: (rows(b, i, s), c_blk + s)),
        pl.BlockSpec((lt, LANES), lambda b, i, s: (rows(b, i, s), 0)),
        pl.BlockSpec((lt, D_MODEL), lambda b, i, s: (rows(b, i, s), 0)),
        pl.BlockSpec((CONV_WIDTH, xw), lambda b, i, s: (0, s)),
        pl.BlockSpec((CONV_WIDTH, sw), lambda b, i, s: (0, cw_b_blk + s)),
        pl.BlockSpec((CONV_WIDTH, sw), lambda b, i, s: (0, cw_c_blk + s)),
        pl.BlockSpec((1, xw), lambda b, i, s: (0, s)),
        pl.BlockSpec((1, sw), lambda b, i, s: (0, cw_b_blk + s)),
        pl.BlockSpec((1, sw), lambda b, i, s: (0, cw_c_blk + s)),
        pl.BlockSpec((1, LANES), lambda b, i, s: (0, 0)),
        pl.BlockSpec((1, LANES), lambda b, i, s: (0, 0)),
        pl.BlockSpec((1, xw), lambda b, i, s: (0, s)),
        pl.BlockSpec((1, xw), lambda b, i, s: (0, s)),
        pl.BlockSpec((xw, D_MODEL), lambda b, i, s: (s, 0)),
    ]
    return pl.pallas_call(
        functools.partial(_mamba_core_kernel, lt=lt, lc=lc, gps=gps),
        grid=(batch, nt, SSM_GROUPS // gps),
        in_specs=in_specs,
        out_specs=pl.BlockSpec((lt, D_MODEL), lambda b, i, s: (rows(b, i, s), 0)),
        out_shape=jax.ShapeDtypeStruct((t, D_MODEL), F32),
        scratch_shapes=[
            pltpu.VMEM((SSM_GROUPS, D_STATE, GROUP_CH), F32),
            pltpu.VMEM((SSM_GROUPS, SUBLANES, GROUP_CH), F32),
            pltpu.VMEM((SSM_GROUPS, SUBLANES, D_STATE), F32),
            pltpu.VMEM((SSM_GROUPS, SUBLANES, D_STATE), F32),
            pltpu.VMEM((gps, lt + SUBLANES, GROUP_CH), F32),
            pltpu.VMEM((gps, lt + SUBLANES, D_STATE), F32),
            pltpu.VMEM((gps, lt + SUBLANES, D_STATE), F32),
            pltpu.VMEM((lt, LANES), F32),
            pltpu.VMEM((lt, LANES), F32),
            pltpu.VMEM((LANES, lt), F32),
        ],
        compiler_params=_compiler_params(("parallel", "arbitrary", "arbitrary")),
        name="mamba_core",
    )(proj, proj, proj, proj, dt_raw, x_res,
      conv_w, conv_w, conv_w, conv_b, conv_b, conv_b,
      dt_bias, a_log, d_ch, norm_w, w_out)


def _softplus2(z2):
    neg_abs = lax.bitcast_convert_type(
        lax.bitcast_convert_type(z2, jnp.uint32) | jnp.uint32(0x80000000), F32)
    return jnp.maximum(z2, 0.0) + jnp.log(1.0 + jnp.exp2(neg_abs)) * LOG2E


def _sb_attn_kernel(zmax_ref, q_ref, k_ref, v_ref, o_ref, qs_ref, carry_ref, acc_ref, zs_ref, *,
                    blk, n_sub):
    lane = lax.broadcasted_iota(jnp.int32, (1, LANES), 1)
    first_head = lane < SB_HEAD_DIM
    row = lax.broadcasted_iota(jnp.int32, (blk, blk), 0)
    col = lax.broadcasted_iota(jnp.int32, (blk, blk), 1)
    neg_suffix = jnp.where(row >= col, -1.0, 0.0).astype(BF16)
    row2 = lax.broadcasted_iota(jnp.int32, (2 * blk, blk), 0) & (blk - 1)
    col2 = lax.broadcasted_iota(jnp.int32, (2 * blk, blk), 1)
    diag_causal = col2 < row2

    def logits(q_rows, j):
        kj = k_ref[0, pl.ds(pl.multiple_of(j * blk, blk), blk), :]
        return lax.dot_general(q_rows, kj, (((1,), (1,)), ((), ())), preferred_element_type=F32)

    def weigh(z, j, carry, mask):
        vj = v_ref[0, pl.ds(pl.multiple_of(j * blk, blk), blk), :]
        sp = _softplus2(z)
        if mask is not None:
            sp = jnp.where(mask, sp, 0.0)
        incl = jnp.dot(sp.astype(BF16), neg_suffix, preferred_element_type=F32)
        w = jnp.exp2(z + incl + carry)
        if mask is not None:
            w = jnp.where(mask, w, 0.0)
        w = w.astype(BF16)
        lhs = jnp.concatenate(
            [jnp.concatenate([w[r:r + blk], w[r + blk:r + 2 * blk]], axis=1)
             for r in range(0, z.shape[0], 2 * blk)], axis=0)
        v_heads = jnp.concatenate([jnp.where(first_head, vj, jnp.zeros_like(vj)),
                                   jnp.where(first_head, jnp.zeros_like(vj), vj)], axis=0)
        return carry + incl[:, 0:1], jnp.dot(lhs, v_heads, preferred_element_type=F32)

    def diagonal(sub):
        qi = pl.program_id(2) * n_sub + sub
        qs, carry_s, acc_s, zs = qs_ref.at[sub], carry_ref.at[sub], acc_ref.at[sub], zs_ref.at[sub]
        q = q_ref[0, sub * 2 * blk:(sub + 1) * 2 * blk, :]
        zero = jnp.zeros_like(q)
        q0 = jnp.where(first_head, q, zero)
        q1 = jnp.where(first_head, zero, q)
        qs[...] = jnp.concatenate([q0[:blk], q1[:blk], q0[blk:], q1[blk:]], axis=0)
        top = qs[0:2 * blk, :]
        bottom = qs[2 * blk:4 * blk, :]
        no_carry = jnp.zeros((2 * blk, 1), F32)
        jd = 2 * qi
        c_bot, a_bot = weigh(logits(bottom, jd + 1), jd + 1, no_carry, diag_causal)
        c_bot, a_bot2 = weigh(logits(bottom, jd), jd, c_bot, None)
        c_top, a_top = weigh(logits(top, jd), jd, no_carry, diag_causal)
        carry_s[0:2 * blk, :] = c_top
        carry_s[2 * blk:4 * blk, :] = c_bot
        acc_s[0:blk, :] = a_top
        acc_s[blk:2 * blk, :] = a_bot + a_bot2
        zs[...] = logits(qs[...], jnp.maximum(jd - 1, 0))

    def walk(sub):
        qi = pl.program_id(2) * n_sub + sub
        qs, carry_s, acc_s, zs = qs_ref.at[sub], carry_ref.at[sub], acc_ref.at[sub], zs_ref.at[sub]
        jd = 2 * qi

        def some_weight_may_be_nonzero():
            return (jnp.max(carry_s[...]) + zmax_ref[0] > -UNDERFLOW_LOG2).astype(jnp.int32)

        def visit_blocks(n_iters, n_blocks, first_block):
            def step(state):
                t, _ = state
                j = first_block - n_blocks * t
                z = zs[...]
                c = carry_s[...]
                total = None
                for u in range(n_blocks):
                    if u + 1 < n_blocks:
                        z_next = logits(qs[...], j - u - 1)
                    else:
                        zs[...] = logits(qs[...], jnp.maximum(j - n_blocks, 0))
                    c, a = weigh(z, j - u, c, None)
                    total = a if total is None else total + a
                    if u + 1 < n_blocks:
                        z = z_next
                carry_s[...] = c
                acc_s[...] += total
                return t + 1, some_weight_may_be_nonzero()

            lax.while_loop(lambda state: (state[0] < n_iters) & (state[1] != 0), step,
                           (jnp.int32(0), some_weight_may_be_nonzero()))

        n_quads = jnp.minimum(lax.shift_right_logical(qi, 1), 1)
        visit_blocks(n_quads, 4, jd - 1)
        visit_blocks(jd - 4 * n_quads, 1, jd - 1 - 4 * n_quads)
        o_ref[0, sub * 2 * blk:(sub + 1) * 2 * blk, :] = acc_s[...].astype(o_ref.dtype)

    for sub in range(n_sub):
        diagonal(sub)
    for sub in range(n_sub):
        walk(sub)


def _sb_attention(zmax, qkv, *, blk, n_sub):
    b, l, _ = qkv.shape
    pairs = SB_HEADS * SB_HEAD_DIM // LANES
    rows = 2 * blk * n_sub
    return pl.pallas_call(
        functools.partial(_sb_attn_kernel, blk=blk, n_sub=n_sub),
        grid=(b, pairs, l // rows),
        in_specs=[
            pl.BlockSpec(memory_space=pltpu.SMEM),
            pl.BlockSpec((1, rows, LANES), lambda bi, p, i: (bi, i, p)),
            pl.BlockSpec((1, l, LANES), lambda bi, p, i: (bi, 0, pairs + p)),
            pl.BlockSpec((1, l, LANES), lambda bi, p, i: (bi, 0, 2 * pairs + p)),
        ],
        out_specs=pl.BlockSpec((1, rows, LANES), lambda bi, p, i: (bi, i, p)),
        out_shape=jax.ShapeDtypeStruct((b, l, D_MODEL), BF16),
        scratch_shapes=[
            pltpu.VMEM((n_sub, 4 * blk, LANES), BF16),
            pltpu.VMEM((n_sub, 4 * blk, 1), F32),
            pltpu.VMEM((n_sub, 2 * blk, LANES), F32),
            pltpu.VMEM((n_sub, 4 * blk, blk), F32),
        ],
        compiler_params=_compiler_params(("parallel", "parallel", "arbitrary")),
        name="sb_attention",
    )(zmax, qkv, qkv, qkv)


def _proj_residual_kernel(x_ref, a_ref, w_ref, o_ref):
    o_ref[...] = x_ref[...] + jnp.dot(a_ref[...], w_ref[...], preferred_element_type=F32)


def _proj_residual(x, a, w, *, tm):
    t, d = x.shape
    k = a.shape[1]
    return pl.pallas_call(
        _proj_residual_kernel,
        grid=(t // tm,),
        in_specs=[
            pl.BlockSpec((tm, d), lambda i: (i, 0)),
            pl.BlockSpec((tm, k), lambda i: (i, 0)),
            pl.BlockSpec((k, d), lambda i: (0, 0)),
        ],
        out_specs=pl.BlockSpec((tm, d), lambda i: (i, 0)),
        out_shape=jax.ShapeDtypeStruct((t, d), F32),
        compiler_params=_compiler_params(("parallel",)),
        name="proj_residual",
    )(x, a, w)


def _ffn_kernel(x_ref, g_ref, wg_ref, wu_ref, wo_ref, o_ref, h_ref):
    j = pl.program_id(1)

    @pl.when(j == 0)
    def _():
        x = x_ref[...]
        h_ref[...] = (_rms(x, NORM_EPS) * g_ref[...]).astype(BF16)
        o_ref[...] = x

    h = h_ref[...]
    gate = jnp.dot(h, wg_ref[...], preferred_element_type=F32)
    up = jnp.dot(h, wu_ref[...], preferred_element_type=F32)
    act = (_silu(gate) * up).astype(BF16)
    o_ref[...] += jnp.dot(act, wo_ref[...], preferred_element_type=F32)


def _ffn(x, gain, w_in, w_out, *, tm, tf):
    t, d = x.shape
    nf = D_FF // tf
    weights_mode = pl.Buffered(1) if nf == 1 else None
    return pl.pallas_call(
        _ffn_kernel,
        grid=(t // tm, nf),
        in_specs=[
            pl.BlockSpec((tm, d), lambda i, j: (i, 0)),
            pl.BlockSpec((1, d), lambda i, j: (0, 0)),
            pl.BlockSpec((d, tf), lambda i, j: (0, j), pipeline_mode=weights_mode),
            pl.BlockSpec((d, tf), lambda i, j: (0, nf + j), pipeline_mode=weights_mode),
            pl.BlockSpec((tf, d), lambda i, j: (j, 0), pipeline_mode=weights_mode),
        ],
        out_specs=pl.BlockSpec((tm, d), lambda i, j: (i, 0)),
        out_shape=jax.ShapeDtypeStruct((t, d), F32),
        scratch_shapes=[pltpu.VMEM((tm, d), BF16)],
        compiler_params=_compiler_params(("parallel", "arbitrary")),
        name="swiglu_ffn",
    )(x, gain, w_in, w_in, w_out)


def _tiles(batch, seq):
    t = batch * seq
    tm = min(1024, t)
    return dict(
        proj_tm=tm,
        in_proj_tm=min(2048, t),
        in_proj_tn=1024,
        qkv_tn=D_MODEL,
        mamba_lt=min(256, seq),
        mamba_lc=min(128, seq),
        mamba_gps=4,
        attn_blk=min(256, seq),
        attn_sub=2 if seq % (4 * min(256, seq)) == 0 else 1,
        ffn_tm=min(512, t),
        ffn_tf=D_FF,
    )


def kernel(x, norm_mix, norm_ffn, ssm_w_in, ssm_conv_w, ssm_conv_b, ssm_dt_bias, ssm_a_log, ssm_d,
           ssm_norm_w, ssm_w_out, sb_w_qkv, sb_q_gain, sb_k_gain, sb_w_o, ffn_w_in, ffn_w_out):
    batch, seq, d = x.shape
    t = batch * seq
    tl = _tiles(batch, seq)
    xf = x.reshape(t, d)

    pad_heads = LANES - SSM_HEADS
    w_in = ssm_w_in[0].astype(BF16)
    proj, dt_raw = _in_proj(xf, norm_mix[0:1], w_in[:, :IN_PROJ_COLS],
                            jnp.pad(w_in[:, IN_PROJ_COLS:], ((0, 0), (0, pad_heads))),
                            tm=tl["in_proj_tm"], tn=tl["in_proj_tn"])
    xf = _mamba_core(
        proj, dt_raw, xf, ssm_conv_w[0], ssm_conv_b[0:1],
        jnp.pad(ssm_dt_bias[0:1], ((0, 0), (0, pad_heads))),
        jnp.pad(ssm_a_log[0:1], ((0, 0), (0, pad_heads))),
        jnp.repeat(ssm_d[0:1], SSM_HEAD_DIM, axis=1),
        ssm_norm_w[0:1], ssm_w_out[0].astype(BF16),
        batch=batch, lt=tl["mamba_lt"], lc=tl["mamba_lc"], gps=tl["mamba_gps"])
    xf = _ffn(xf, norm_ffn[0:1], ffn_w_in[0].astype(BF16), ffn_w_out[0].astype(BF16),
              tm=tl["ffn_tm"], tf=tl["ffn_tf"])

    head_gain = jnp.concatenate([
        jnp.tile(sb_q_gain[0], SB_HEADS) * (LOG2E * SB_HEAD_DIM ** -0.5),
        jnp.tile(sb_k_gain[0], SB_HEADS),
        jnp.ones((D_MODEL,), F32)])[None, :]
    qkv = _qkv_proj(xf, norm_mix[1:2], sb_w_qkv[0].astype(BF16), head_gain,
                    tm=tl["proj_tm"], tn=tl["qkv_tn"])
    zmax = (1.1 * LOG2E * SB_HEAD_DIM ** 0.5) * jnp.max(jnp.abs(sb_q_gain[0])) * jnp.max(jnp.abs(sb_k_gain[0]))
    attn = _sb_attention(zmax.reshape(1), qkv.reshape(batch, seq, 3 * d), blk=tl["attn_blk"],
                         n_sub=tl["attn_sub"])
    xf = _proj_residual(xf, attn.reshape(t, d), sb_w_o[0].astype(BF16), tm=tl["proj_tm"])
    xf = _ffn(xf, norm_ffn[1:2], ffn_w_in[1].astype(BF16), ffn_w_out[1].astype(BF16),
              tm=tl["ffn_tm"], tf=tl["ffn_tf"])
    return xf.reshape(batch, seq, d)
```

```python
import functools

import jax
import jax.numpy as jnp
from jax import lax
from jax.experimental import pallas as pl
from jax.experimental.pallas import tpu as pltpu

F32 = jnp.float32
BF16 = jnp.bfloat16

D_MODEL = 1024
SSM_HEAD_DIM = 64
SSM_HEADS = 32
SSM_GROUPS = 8
SSM_HEADS_PER_GROUP = SSM_HEADS // SSM_GROUPS
D_STATE = 128
D_INNER = SSM_HEADS * SSM_HEAD_DIM
GROUP_CH = D_INNER // SSM_GROUPS
CONV_WIDTH = 4
SB_HEADS = 16
SB_HEAD_DIM = 64
D_FF = 2816
NORM_EPS = 1e-6
GATED_NORM_EPS = 1e-5
LOG2E = 1.4426950408889634
UNDERFLOW_LOG2 = 1100.0

LANES = 128
SUBLANES = 8
MXU_DIM = 256
VMEM_LIMIT_BYTES = 56 * 1024 * 1024

IN_PROJ_COLS = D_INNER + D_INNER + 2 * SSM_GROUPS * D_STATE


def _rms(x, eps):
    return x * lax.rsqrt(jnp.mean(x * x, axis=-1, keepdims=True) + eps)


def _softplus(x):
    return jnp.maximum(x, 0.0) + jnp.log(1.0 + jnp.exp(-jnp.abs(x)))


def _silu(x):
    return x / (1.0 + jnp.exp(-x))


def _compiler_params(semantics):
    return pltpu.CompilerParams(dimension_semantics=semantics, vmem_limit_bytes=VMEM_LIMIT_BYTES)


def _in_proj_kernel(x_ref, g_ref, w_ref, wdt_ref, o_ref, odt_ref, h_ref):
    @pl.when(pl.program_id(1) == 0)
    def _():
        h = (_rms(x_ref[...], NORM_EPS) * g_ref[...]).astype(BF16)
        h_ref[...] = h
        odt_ref[...] = jnp.dot(h, wdt_ref[...], preferred_element_type=F32)

    o_ref[...] = jnp.dot(h_ref[...], w_ref[...], preferred_element_type=F32)


def _in_proj(x, gain, w, w_dt, *, tm, tn):
    t, d = x.shape
    n = w.shape[1]
    return pl.pallas_call(
        _in_proj_kernel,
        grid=(t // tm, n // tn),
        in_specs=[
            pl.BlockSpec((tm, d), lambda i, j: (i, 0)),
            pl.BlockSpec((1, d), lambda i, j: (0, 0)),
            pl.BlockSpec((d, tn), lambda i, j: (0, j)),
            pl.BlockSpec((d, LANES), lambda i, j: (0, 0)),
        ],
        out_specs=[
            pl.BlockSpec((tm, tn), lambda i, j: (i, j)),
            pl.BlockSpec((tm, LANES), lambda i, j: (i, 0)),
        ],
        out_shape=[jax.ShapeDtypeStruct((t, n), F32), jax.ShapeDtypeStruct((t, LANES), F32)],
        scratch_shapes=[pltpu.VMEM((tm, d), BF16)],
        compiler_params=_compiler_params(("parallel", "arbitrary")),
        name="in_proj",
    )(x, gain, w, w_dt)


def _qkv_proj_kernel(x_ref, g_ref, w_ref, hg_ref, o_ref, h_ref, *, n_norm_tiles):
    j = pl.program_id(1)

    @pl.when(j == 0)
    def _():
        h_ref[...] = (_rms(x_ref[...], NORM_EPS) * g_ref[...]).astype(BF16)

    y = jnp.dot(h_ref[...], w_ref[...], preferred_element_type=F32)

    @pl.when(j < n_norm_tiles)
    def _():
        row = lax.broadcasted_iota(jnp.int32, (MXU_DIM, MXU_DIM), 0) // SB_HEAD_DIM
        col = lax.broadcasted_iota(jnp.int32, (MXU_DIM, MXU_DIM), 1) // SB_HEAD_DIM
        head_mean = jnp.where(row == col, 1.0 / SB_HEAD_DIM, 0.0).astype(BF16)
        y2 = (y * y).astype(BF16)
        ms = jnp.concatenate(
            [jnp.dot(y2[:, c:c + MXU_DIM], head_mean, preferred_element_type=F32)
             for c in range(0, y.shape[1], MXU_DIM)], axis=1)
        o_ref[...] = (y * lax.rsqrt(ms + NORM_EPS) * hg_ref[...]).astype(o_ref.dtype)

    @pl.when(j >= n_norm_tiles)
    def _():
        o_ref[...] = y.astype(o_ref.dtype)


def _qkv_proj(x, gain, w, head_gain, *, tm, tn):
    t, d = x.shape
    n = w.shape[1]
    return pl.pallas_call(
        functools.partial(_qkv_proj_kernel, n_norm_tiles=(2 * D_MODEL) // tn),
        grid=(t // tm, n // tn),
        in_specs=[
            pl.BlockSpec((tm, d), lambda i, j: (i, 0)),
            pl.BlockSpec((1, d), lambda i, j: (0, 0)),
            pl.BlockSpec((d, tn), lambda i, j: (0, j)),
            pl.BlockSpec((1, tn), lambda i, j: (0, j)),
        ],
        out_specs=pl.BlockSpec((tm, tn), lambda i, j: (i, j)),
        out_shape=jax.ShapeDtypeStruct((t, n), BF16),
        scratch_shapes=[pltpu.VMEM((tm, d), BF16)],
        compiler_params=_compiler_params(("parallel", "arbitrary")),
        name="qkv_proj",
    )(x, gain, w, head_gain)


def _conv_silu(u_ref, tail_ref, buf_ref, w_ref, b_ref, g, k_local, width, lt):
    lanes = slice(k_local * width, (k_local + 1) * width)
    buf = buf_ref.at[k_local]
    buf[0:SUBLANES, :] = tail_ref[g]
    buf[SUBLANES:SUBLANES + lt, :] = u_ref[:, lanes]
    tail_ref[g] = buf[lt:lt + SUBLANES, :]
    acc = b_ref[:, lanes]
    for k in range(CONV_WIDTH):
        start = SUBLANES - (CONV_WIDTH - 1) + k
        acc = acc + w_ref[k:k + 1, lanes] * buf[start:start + lt, :]
    return _silu(acc)


def _head_select(pieces):
    lane = lax.broadcasted_iota(jnp.int32, (1, LANES), 1)
    lo = lane < SSM_HEAD_DIM
    return jnp.concatenate(
        [jnp.where(lo, pieces[0], pieces[1]), jnp.where(lo, pieces[2], pieces[3])], axis=1)


def _split3_dot(tri_bf16, a):
    a0 = a.astype(BF16)
    r1 = a - a0.astype(F32)
    a1 = r1.astype(BF16)
    a2 = (r1 - a1.astype(F32)).astype(BF16)
    out = jnp.dot(tri_bf16, a0, preferred_element_type=F32)
    out = out + jnp.dot(tri_bf16, a1, preferred_element_type=F32)
    return out + jnp.dot(tri_bf16, a2, preferred_element_type=F32)


def _mamba_core_kernel(z_ref, ux_ref, ub_ref, uc_ref, dt_ref, xres_ref,
                       cwx_ref, cwb_ref, cwc_ref, cbx_ref, cbb_ref, cbc_ref,
                       dtb_ref, alog_ref, dch_ref, nw_ref, wout_ref,
                       o_ref,
                       state_ref, tailx_ref, tailb_ref, tailc_ref,
                       bufx_ref, bufb_ref, bufc_ref, dts_ref, acs_ref, acst_ref, *, lt, lc, gps):
    i = pl.program_id(1)
    gstep = pl.program_id(2)

    @pl.when(i == 0)
    def _():
        for k_local in range(gps):
            g = gstep * gps + k_local
            state_ref[g] = jnp.zeros(state_ref.shape[1:], F32)
            tailx_ref[g] = jnp.zeros(tailx_ref.shape[1:], F32)
            tailb_ref[g] = jnp.zeros(tailb_ref.shape[1:], F32)
            tailc_ref[g] = jnp.zeros(tailc_ref.shape[1:], F32)

    @pl.when(gstep == 0)
    def _():
        dt_all = _softplus(dt_ref[...] + dtb_ref[...])
        a = dt_all * (-jnp.exp(alog_ref[...]))
        row = lax.broadcasted_iota(jnp.int32, (lt, lt), 0)
        col = lax.broadcasted_iota(jnp.int32, (lt, lt), 1)
        same_chunk_causal = (row >= col) & ((row // lc) == (col // lc))
        acs_all = _split3_dot(same_chunk_causal.astype(BF16), a)
        dts_ref[...] = dt_all
        acs_ref[...] = acs_all
        acst_ref[...] = acs_all.T

    dt = dts_ref[...]
    acs = acs_ref[...]
    lane = lax.broadcasted_iota(jnp.int32, (1, LANES), 1)
    row_c = lax.broadcasted_iota(jnp.int32, (lc, lc), 0)
    col_c = lax.broadcasted_iota(jnp.int32, (lc, lc), 1)
    causal = row_c >= col_c
    second_head = (lane // SSM_HEAD_DIM) == 1
    normed = [_mamba_group(k_local, gstep * gps + k_local, dt, acs, lane, causal, second_head,
                           z_ref, ux_ref, ub_ref, uc_ref, cwx_ref, cwb_ref, cwc_ref,
                           cbx_ref, cbb_ref, cbc_ref, dch_ref, nw_ref,
                           state_ref, tailx_ref, tailb_ref, tailc_ref,
                           bufx_ref, bufb_ref, bufc_ref, acst_ref, lt, lc)
              for k_local in range(gps)]
    contrib = jnp.dot(jnp.concatenate(normed, axis=1), wout_ref[...], preferred_element_type=F32)

    @pl.when(gstep == 0)
    def _():
        o_ref[...] = xres_ref[...] + contrib

    @pl.when(gstep != 0)
    def _():
        o_ref[...] += contrib


def _mamba_group(k_local, g, dt, acs, lane, causal, second_head,
                 z_ref, ux_ref, ub_ref, uc_ref, cwx_ref, cwb_ref, cwc_ref,
                 cbx_ref, cbb_ref, cbc_ref, dch_ref, nw_ref,
                 state_ref, tailx_ref, tailb_ref, tailc_ref,
                 bufx_ref, bufb_ref, bufc_ref, acst_ref, lt, lc):
    ch = slice(k_local * GROUP_CH, (k_local + 1) * GROUP_CH)
    xs = _conv_silu(ux_ref, tailx_ref, bufx_ref, cwx_ref, cbx_ref, g, k_local, GROUP_CH, lt)
    bm = _conv_silu(ub_ref, tailb_ref, bufb_ref, cwb_ref, cbb_ref, g, k_local, D_STATE, lt)
    cm = _conv_silu(uc_ref, tailc_ref, bufc_ref, cwc_ref, cbc_ref, g, k_local, D_STATE, lt)
    bm16 = bm.astype(BF16)
    cm16 = cm.astype(BF16)

    dt_cols, acs_cols, acs_rows = [], [], []
    for r in range(SSM_HEADS_PER_GROUP):
        h = g * SSM_HEADS_PER_GROUP + r
        sel = lane == h
        dt_cols.append(jnp.sum(jnp.where(sel, dt, 0.0), axis=1, keepdims=True))
        acs_cols.append(jnp.sum(jnp.where(sel, acs, 0.0), axis=1, keepdims=True))
        acs_rows.append(acst_ref[pl.ds(h, 1), :])

    dt_x = _head_select(dt_cols)
    acs_x = _head_select(acs_cols)
    xdt = xs * dt_x
    decay_in = jnp.exp(acs_x)

    state = state_ref[g]
    y_chunks = []
    for c in range(lt // lc):
        lo, hi = c * lc, (c + 1) * lc
        cb = lax.dot_general(cm16[lo:hi], bm16[lo:hi], (((1,), (1,)), ((), ())),
                             preferred_element_type=F32)
        y_tiles = [None, None]
        for r in range(SSM_HEADS_PER_GROUP):
            seg = acs_cols[r][lo:hi] - acs_rows[r][:, lo:hi]
            lmat = jnp.exp(jnp.where(causal, seg, -jnp.inf))
            gm = (cb * lmat).astype(BF16)
            tile = r // 2
            x_tile = xdt[lo:hi, tile * LANES:(tile + 1) * LANES]
            keep = second_head if r % 2 else jnp.logical_not(second_head)
            part = jnp.dot(gm, jnp.where(keep, x_tile, 0.0).astype(BF16), preferred_element_type=F32)
            y_tiles[tile] = part if y_tiles[tile] is None else y_tiles[tile] + part
        y_inter = jnp.dot(cm16[lo:hi], state.astype(BF16), preferred_element_type=F32)
        y_chunks.append(jnp.concatenate(y_tiles, axis=1) + y_inter * decay_in[lo:hi])
        last_x = acs_x[hi - 1:hi, :]
        w_end = (xdt[lo:hi] * jnp.exp(last_x - acs_x[lo:hi])).astype(BF16)
        upd = lax.dot_general(bm16[lo:hi], w_end, (((0,), (0,)), ((), ())),
                              preferred_element_type=F32)
        state = state * jnp.exp(last_x) + upd
    state_ref[g] = state
    y = jnp.concatenate(y_chunks, axis=0) + dch_ref[:, ch] * xs

    yg = y * _silu(z_ref[:, ch])
    return (_rms(yg, GATED_NORM_EPS) * nw_ref[:, ch]).astype(BF16)


def _mamba_core(proj, dt_raw, x_res, conv_w, conv_b, dt_bias, a_log, d_ch, norm_w, w_out, *,
                batch, lt, lc, gps):
    t = proj.shape[0]
    nt = t // batch // lt
    xw = gps * GROUP_CH
    sw = gps * D_STATE

    def rows(b, i, s):
        return b * nt + i

    x_blk = D_INNER // xw
    b_blk = 2 * D_INNER // sw
    c_blk = b_blk + SSM_GROUPS * D_STATE // sw
    cw_b_blk = D_INNER // sw
    cw_c_blk = cw_b_blk + SSM_GROUPS * D_STATE // sw
    in_specs = [
        pl.BlockSpec((lt, xw), lambda b, i, s: (rows(b, i, s), s)),
        pl.BlockSpec((lt, xw), lambda b, i, s: (rows(b, i, s), x_blk + s)),
        pl.BlockSpec((lt, sw), lambda b, i, s: (rows(b, i, s), b_blk + s)),
        pl.BlockSpec((lt, sw), lambda b, i, s: (rows(b, i, s), c_blk + s)),
        pl.BlockSpec((lt, LANES), lambda b, i, s: (rows(b, i, s), 0)),
        pl.BlockSpec((lt, D_MODEL), lambda b, i, s: (rows(b, i, s), 0)),
        pl.BlockSpec((CONV_WIDTH, xw), lambda b, i, s: (0, s)),
        pl.BlockSpec((CONV_WIDTH, sw), lambda b, i, s: (0, cw_b_blk + s)),
        pl.BlockSpec((CONV_WIDTH, sw), lambda b, i, s: (0, cw_c_blk + s)),
        pl.BlockSpec((1, xw), lambda b, i, s: (0, s)),
        pl.BlockSpec((1, sw), lambda b, i, s: (0, cw_b_blk + s)),
        pl.BlockSpec((1, sw), lambda b, i, s: (0, cw_c_blk + s)),
        pl.BlockSpec((1, LANES), lambda b, i, s: (0, 0)),
        pl.BlockSpec((1, LANES), lambda b, i, s: (0, 0)),
        pl.BlockSpec((1, xw), lambda b, i, s: (0, s)),
        pl.BlockSpec((1, xw), lambda b, i, s: (0, s)),
        pl.BlockSpec((xw, D_MODEL), lambda b, i, s: (s, 0)),
    ]
    return pl.pallas_call(
        functools.partial(_mamba_core_kernel, lt=lt, lc=lc, gps=gps),
        grid=(batch, nt, SSM_GROUPS // gps),
        in_specs=in_specs,
        out_specs=pl.BlockSpec((lt, D_MODEL), lambda b, i, s: (rows(b, i, s), 0)),
        out_shape=jax.ShapeDtypeStruct((t, D_MODEL), F32),
        scratch_shapes=[
            pltpu.VMEM((SSM_GROUPS, D_STATE, GROUP_CH), F32),
            pltpu.VMEM((SSM_GROUPS, SUBLANES, GROUP_CH), F32),
            pltpu.VMEM((SSM_GROUPS, SUBLANES, D_STATE), F32),
            pltpu.VMEM((SSM_GROUPS, SUBLANES, D_STATE), F32),
            pltpu.VMEM((gps, lt + SUBLANES, GROUP_CH), F32),
            pltpu.VMEM((gps, lt + SUBLANES, D_STATE), F32),
            pltpu.VMEM((gps, lt + SUBLANES, D_STATE), F32),
            pltpu.VMEM((lt, LANES), F32),
            pltpu.VMEM((lt, LANES), F32),
            pltpu.VMEM((LANES, lt), F32),
        ],
        compiler_params=_compiler_params(("parallel", "arbitrary", "arbitrary")),
        name="mamba_core",
    )(proj, proj, proj, proj, dt_raw, x_res,
      conv_w, conv_w, conv_w, conv_b, conv_b, conv_b,
      dt_bias, a_log, d_ch, norm_w, w_out)


def _softplus2(z2):
    neg_abs = lax.bitcast_convert_type(
        lax.bitcast_convert_type(z2, jnp.uint32) | jnp.uint32(0x80000000), F32)
    return jnp.maximum(z2, 0.0) + jnp.log(1.0 + jnp.exp2(neg_abs)) * LOG2E


def _sb_attn_kernel(zmax_ref, q_ref, k_ref, v_ref, o_ref, qs_ref, carry_ref, acc_ref, zs_ref, *,
                    blk, n_sub):
    lane = lax.broadcasted_iota(jnp.int32, (1, LANES), 1)
    first_head = lane < SB_HEAD_DIM
    row = lax.broadcasted_iota(jnp.int32, (blk, blk), 0)
    col = lax.broadcasted_iota(jnp.int32, (blk, blk), 1)
    neg_suffix = jnp.where(row >= col, -1.0, 0.0).astype(BF16)
    row2 = lax.broadcasted_iota(jnp.int32, (2 * blk, blk), 0) & (blk - 1)
    col2 = lax.broadcasted_iota(jnp.int32, (2 * blk, blk), 1)
    diag_causal = col2 < row2

    def logits(q_rows, j):
        kj = k_ref[0, pl.ds(pl.multiple_of(j * blk, blk), blk), :]
        return lax.dot_general(q_rows, kj, (((1,), (1,)), ((), ())), preferred_element_type=F32)

    def weigh(z, j, carry, mask):
        vj = v_ref[0, pl.ds(pl.multiple_of(j * blk, blk), blk), :]
        sp = _softplus2(z)
        if mask is not None:
            sp = jnp.where(mask, sp, 0.0)
        incl = jnp.dot(sp.astype(BF16), neg_suffix, preferred_element_type=F32)
        w = jnp.exp2(z + incl + carry)
        if mask is not None:
            w = jnp.where(mask, w, 0.0)
        w = w.astype(BF16)
        lhs = jnp.concatenate(
            [jnp.concatenate([w[r:r + blk], w[r + blk:r + 2 * blk]], axis=1)
             for r in range(0, z.shape[0], 2 * blk)], axis=0)
        v_heads = jnp.concatenate([jnp.where(first_head, vj, jnp.zeros_like(vj)),
                                   jnp.where(first_head, jnp.zeros_like(vj), vj)], axis=0)
        return carry + incl[:, 0:1], jnp.dot(lhs, v_heads, preferred_element_type=F32)

    def diagonal(sub):
        qi = pl.program_id(2) * n_sub + sub
        qs, carry_s, acc_s, zs = qs_ref.at[sub], carry_ref.at[sub], acc_ref.at[sub], zs_ref.at[sub]
        q = q_ref[0, sub * 2 * blk:(sub + 1) * 2 * blk, :]
        zero = jnp.zeros_like(q)
        q0 = jnp.where(first_head, q, zero)
        q1 = jnp.where(first_head, zero, q)
        qs[...] = jnp.concatenate([q0[:blk], q1[:blk], q0[blk:], q1[blk:]], axis=0)
        top = qs[0:2 * blk, :]
        bottom = qs[2 * blk:4 * blk, :]
        no_carry = jnp.zeros((2 * blk, 1), F32)
        jd = 2 * qi
        c_bot, a_bot = weigh(logits(bottom, jd + 1), jd + 1, no_carry, diag_causal)
        c_bot, a_bot2 = weigh(logits(bottom, jd), jd, c_bot, None)
        c_top, a_top = weigh(logits(top, jd), jd, no_carry, diag_causal)
        carry_s[0:2 * blk, :] = c_top
        carry_s[2 * blk:4 * blk, :] = c_bot
        acc_s[0:blk, :] = a_top
        acc_s[blk:2 * blk, :] = a_bot + a_bot2
        zs[...] = logits(qs[...], jnp.maximum(jd - 1, 0))

    def walk(sub):
        qi = pl.program_id(2) * n_sub + sub
        qs, carry_s, acc_s, zs = qs_ref.at[sub], carry_ref.at[sub], acc_ref.at[sub], zs_ref.at[sub]
        jd = 2 * qi

        def some_weight_may_be_nonzero():
            return (jnp.max(carry_s[...]) + zmax_ref[0] > -UNDERFLOW_LOG2).astype(jnp.int32)

        def visit_blocks(n_iters, n_blocks, first_block):
            def step(state):
                t, _ = state
                j = first_block - n_blocks * t
                z = zs[...]
                c = carry_s[...]
                total = None
                for u in range(n_blocks):
                    if u + 1 < n_blocks:
                        z_next = logits(qs[...], j - u - 1)
                    else:
                        zs[...] = logits(qs[...], jnp.maximum(j - n_blocks, 0))
                    c, a = weigh(z, j - u, c, None)
                    total = a if total is None else total + a
                    if u + 1 < n_blocks:
                        z = z_next
                carry_s[...] = c
                acc_s[...] += total
                return t + 1, some_weight_may_be_nonzero()

            lax.while_loop(lambda state: (state[0] < n_iters) & (state[1] != 0), step,
                           (jnp.int32(0), some_weight_may_be_nonzero()))

        n_quads = jnp.minimum(lax.shift_right_logical(qi, 1), 1)
        visit_blocks(n_quads, 4, jd - 1)
        visit_blocks(jd - 4 * n_quads, 1, jd - 1 - 4 * n_quads)
        o_ref[0, sub * 2 * blk:(sub + 1) * 2 * blk, :] = acc_s[...].astype(o_ref.dtype)

    for sub in range(n_sub):
        diagonal(sub)
    for sub in range(n_sub):
        walk(sub)


def _sb_attention(zmax, qkv, *, blk, n_sub):
    b, l, _ = qkv.shape
    pairs = SB_HEADS * SB_HEAD_DIM // LANES
    rows = 2 * blk * n_sub
    return pl.pallas_call(
        functools.partial(_sb_attn_kernel, blk=blk, n_sub=n_sub),
        grid=(b, pairs, l // rows),
        in_specs=[
            pl.BlockSpec(memory_space=pltpu.SMEM),
            pl.BlockSpec((1, rows, LANES), lambda bi, p, i: (bi, i, p)),
            pl.BlockSpec((1, l, LANES), lambda bi, p, i: (bi, 0, pairs + p)),
            pl.BlockSpec((1, l, LANES), lambda bi, p, i: (bi, 0, 2 * pairs + p)),
        ],
        out_specs=pl.BlockSpec((1, rows, LANES), lambda bi, p, i: (bi, i, p)),
        out_shape=jax.ShapeDtypeStruct((b, l, D_MODEL), BF16),
        scratch_shapes=[
            pltpu.VMEM((n_sub, 4 * blk, LANES), BF16),
            pltpu.VMEM((n_sub, 4 * blk, 1), F32),
            pltpu.VMEM((n_sub, 2 * blk, LANES), F32),
            pltpu.VMEM((n_sub, 4 * blk, blk), F32),
        ],
        compiler_params=_compiler_params(("parallel", "parallel", "arbitrary")),
        name="sb_attention",
    )(zmax, qkv, qkv, qkv)


def _proj_residual_kernel(x_ref, a_ref, w_ref, o_ref):
    o_ref[...] = x_ref[...] + jnp.dot(a_ref[...], w_ref[...], preferred_element_type=F32)


def _proj_residual(x, a, w, *, tm):
    t, d = x.shape
    k = a.shape[1]
    return pl.pallas_call(
        _proj_residual_kernel,
        grid=(t // tm,),
        in_specs=[
            pl.BlockSpec((tm, d), lambda i: (i, 0)),
            pl.BlockSpec((tm, k), lambda i: (i, 0)),
            pl.BlockSpec((k, d), lambda i: (0, 0)),
        ],
        out_specs=pl.BlockSpec((tm, d), lambda i: (i, 0)),
        out_shape=jax.ShapeDtypeStruct((t, d), F32),
        compiler_params=_compiler_params(("parallel",)),
        name="proj_residual",
    )(x, a, w)


def _ffn_kernel(x_ref, g_ref, wg_ref, wu_ref, wo_ref, o_ref, h_ref):
    j = pl.program_id(1)

    @pl.when(j == 0)
    def _():
        x = x_ref[...]
        h_ref[...] = (_rms(x, NORM_EPS) * g_ref[...]).astype(BF16)
        o_ref[...] = x

    h = h_ref[...]
    gate = jnp.dot(h, wg_ref[...], preferred_element_type=F32)
    up = jnp.dot(h, wu_ref[...], preferred_element_type=F32)
    act = (_silu(gate) * up).astype(BF16)
    o_ref[...] += jnp.dot(act, wo_ref[...], preferred_element_type=F32)


def _ffn(x, gain, w_in, w_out, *, tm, tf):
    t, d = x.shape
    nf = D_FF // tf
    weights_mode = pl.Buffered(1) if nf == 1 else None
    return pl.pallas_call(
        _ffn_kernel,
        grid=(t // tm, nf),
        in_specs=[
            pl.BlockSpec((tm, d), lambda i, j: (i, 0)),
            pl.BlockSpec((1, d), lambda i, j: (0, 0)),
            pl.BlockSpec((d, tf), lambda i, j: (0, j), pipeline_mode=weights_mode),
            pl.BlockSpec((d, tf), lambda i, j: (0, nf + j), pipeline_mode=weights_mode),
            pl.BlockSpec((tf, d), lambda i, j: (j, 0), pipeline_mode=weights_mode),
        ],
        out_specs=pl.BlockSpec((tm, d), lambda i, j: (i, 0)),
        out_shape=jax.ShapeDtypeStruct((t, d), F32),
        scratch_shapes=[pltpu.VMEM((tm, d), BF16)],
        compiler_params=_compiler_params(("parallel", "arbitrary")),
        name="swiglu_ffn",
    )(x, gain, w_in, w_in, w_out)


def _tiles(batch, seq):
    t = batch * seq
    tm = min(1024, t)
    return dict(
        proj_tm=tm,
        in_proj_tm=min(2048, t),
        in_proj_tn=1024,
        qkv_tn=D_MODEL,
        mamba_lt=min(256, seq),
        mamba_lc=min(128, seq),
        mamba_gps=8,
        attn_blk=min(256, seq),
        attn_sub=2 if seq % (4 * min(256, seq)) == 0 else 1,
        ffn_tm=min(512, t),
        ffn_tf=D_FF,
    )


def kernel(x, norm_mix, norm_ffn, ssm_w_in, ssm_conv_w, ssm_conv_b, ssm_dt_bias, ssm_a_log, ssm_d,
           ssm_norm_w, ssm_w_out, sb_w_qkv, sb_q_gain, sb_k_gain, sb_w_o, ffn_w_in, ffn_w_out):
    batch, seq, d = x.shape
    t = batch * seq
    tl = _tiles(batch, seq)
    xf = x.reshape(t, d)

    pad_heads = LANES - SSM_HEADS
    w_in = ssm_w_in[0].astype(BF16)
    proj, dt_raw = _in_proj(xf, norm_mix[0:1], w_in[:, :IN_PROJ_COLS],
                            jnp.pad(w_in[:, IN_PROJ_COLS:], ((0, 0), (0, pad_heads))),
                            tm=tl["in_proj_tm"], tn=tl["in_proj_tn"])
    xf = _mamba_core(
        proj, dt_raw, xf, ssm_conv_w[0], ssm_conv_b[0:1],
        jnp.pad(ssm_dt_bias[0:1], ((0, 0), (0, pad_heads))),
        jnp.pad(ssm_a_log[0:1], ((0, 0), (0, pad_heads))),
        jnp.repeat(ssm_d[0:1], SSM_HEAD_DIM, axis=1),
        ssm_norm_w[0:1], ssm_w_out[0].astype(BF16),
        batch=batch, lt=tl["mamba_lt"], lc=tl["mamba_lc"], gps=tl["mamba_gps"])
    xf = _ffn(xf, norm_ffn[0:1], ffn_w_in[0].astype(BF16), ffn_w_out[0].astype(BF16),
              tm=tl["ffn_tm"], tf=tl["ffn_tf"])

    head_gain = jnp.concatenate([
        jnp.tile(sb_q_gain[0], SB_HEADS) * (LOG2E * SB_HEAD_DIM ** -0.5),
        jnp.tile(sb_k_gain[0], SB_HEADS),
        jnp.ones((D_MODEL,), F32)])[None, :]
    qkv = _qkv_proj(xf, norm_mix[1:2], sb_w_qkv[0].astype(BF16), head_gain,
                    tm=tl["proj_tm"], tn=tl["qkv_tn"])
    zmax = (1.1 * LOG2E * SB_HEAD_DIM ** 0.5) * jnp.max(jnp.abs(sb_q_gain[0])) * jnp.max(jnp.abs(sb_k_gain[0]))
    attn = _sb_attention(zmax.reshape(1), qkv.reshape(batch, seq, 3 * d), blk=tl["attn_blk"],
                         n_sub=tl["attn_sub"])
    xf = _proj_residual(xf, attn.reshape(t, d), sb_w_o[0].astype(BF16), tm=tl["proj_tm"])
    xf = _ffn(xf, norm_ffn[1:2], ffn_w_in[1].astype(BF16), ffn_w_out[1].astype(BF16),
              tm=tl["ffn_tm"], tf=tl["ffn_tf"])
    return xf.reshape(batch, seq, d)
```
